```python
import math
import jax, jax.numpy as jnp
from jax import lax
import numpy as np

D_MODEL = 2048
BATCH = 2
SEQ = 8192
DEPTH = 1
DEC_BATCH = 32
DEC_SEQ = 4
PAST_LEN = 16384
PAGE_SIZE = 128

HEAD_DIM = 128
SB_HEADS = 6
GDN_HEADS = 6
X_HEADS = 4
N_MEM = 256
SB_W = SB_HEADS * HEAD_DIM
GDN_W = GDN_HEADS * HEAD_DIM
X_W = X_HEADS * HEAD_DIM
GDN_CONV_W = 3 * GDN_W
CONV_K = 4
GDN_CHUNK = 64
Q_BLOCK = 128
D_FF = 5632
NORM_EPS = 1e-6
SB_BIAS_INIT = -7.0
IN_SPLITS = (SB_W, SB_W, SB_W, GDN_W, GDN_W, GDN_W, GDN_HEADS, GDN_HEADS, GDN_W, X_W, D_MODEL, D_MODEL, D_MODEL)
IN_W = sum(IN_SPLITS)

kernel_name = 'hybrid_stickbreak_gdn_memory_decode_step'

F32 = jnp.float32


def rmsnorm(x, g):
    xf = x.astype(F32)
    y = xf * lax.rsqrt(jnp.mean(xf * xf, axis=-1, keepdims=True) + NORM_EPS)
    return (y * g.astype(F32)).astype(x.dtype)


def l2norm(x):
    return x * lax.rsqrt(jnp.sum(x * x, axis=-1, keepdims=True) + NORM_EPS)


def swiglu(x, w_in, w_out):
    gate, up = jnp.split(x @ w_in, 2, axis=-1)
    return (jax.nn.silu(gate) * up) @ w_out


def stick_breaking_weights(z, mask):
    log_rem = jnp.where(mask, -jax.nn.softplus(z), 0.0)
    r = lax.cumsum(log_rem, axis=z.ndim - 1, reverse=True)
    return jnp.where(mask, jnp.exp(jnp.where(mask, z + r, 0.0)), 0.0)


def stick_breaking_prompt(q, k, v, bias):
    b, t, h, d = q.shape
    nb = t // Q_BLOCK
    scale = 1.0 / math.sqrt(HEAD_DIM)
    qb = q.reshape(b, nb, Q_BLOCK, h, d).swapaxes(0, 1)
    vf = v.astype(F32)
    k_pos = jnp.arange(t)
    bias_f = bias.astype(F32)[:, None, None]

    def one_block(args):
        q_blk, start = args
        q_pos = start + jnp.arange(Q_BLOCK)
        z = jnp.einsum('bqhd,bkhd->bhqk', q_blk, k, preferred_element_type=F32) * scale + bias_f
        w = stick_breaking_weights(z, k_pos[None, :] < q_pos[:, None])
        return jnp.einsum('bhqk,bkhd->bqhd', w, vf)

    o = lax.map(one_block, (qb, jnp.arange(nb) * Q_BLOCK))
    return o.swapaxes(0, 1).reshape(b, t, h, d)


def stick_breaking_sample(q, k_new, v_new, past_k, past_v, bias):
    t = q.shape[1]
    p = past_k.shape[1]
    scale = 1.0 / math.sqrt(HEAD_DIM)
    z = jnp.concatenate([
        jnp.einsum('bqhd,bkhd->bhqk', q, past_k, preferred_element_type=F32),
        jnp.einsum('bqhd,bkhd->bhqk', q, k_new, preferred_element_type=F32)], axis=-1) * scale
    z = z + bias.astype(F32)[:, None, None]
    q_pos = p + jnp.arange(t)
    k_pos = jnp.arange(p + t)
    w = stick_breaking_weights(z, k_pos[None, :] < q_pos[:, None])
    return (jnp.einsum('bhqk,bkhd->bqhd', w[..., :p], past_v, preferred_element_type=F32)
            + jnp.einsum('bhqk,bkhd->bqhd', w[..., p:], v_new, preferred_element_type=F32))


def causal_short_conv(x, buf, w):
    t = x.shape[1]
    xp = jnp.concatenate([buf.astype(x.dtype), x], axis=1)
    y = sum(xp[:, j:j + t] * w[j] for j in range(CONV_K))
    return jax.nn.silu(y), xp[:, t:]


def gdn_chunked(q, k, v, g, beta, s0):
    b, t, h, dk = q.shape
    dv = v.shape[-1]
    c = GDN_CHUNK
    pad = (-t) % c
    n = (t + pad) // c

    def chunks(a):
        a = jnp.pad(a, [(0, 0), (0, pad)] + [(0, 0)] * (a.ndim - 2))
        a = a.reshape((b, n, c) + a.shape[2:])
        return jnp.moveaxis(a, 3, 1)

    q, k, v, g, beta = chunks(q), chunks(k), chunks(v), chunks(g), chunks(beta)
    gc = jnp.cumsum(g, axis=-1)
    idx = jnp.arange(c)
    strict = idx[:, None] > idx[None, :]
    incl = idx[:, None] >= idx[None, :]
    diff = gc[..., :, None] - gc[..., None, :]
    dec_strict = jnp.where(strict, jnp.exp(jnp.where(strict, diff, 0.0)), 0.0)
    dec_incl = jnp.where(incl, jnp.exp(jnp.where(incl, diff, 0.0)), 0.0)
    kb = k * beta[..., None]
    lower = jnp.einsum('bhnid,bhnjd->bhnij', kb, k) * dec_strict
    a_mat = jnp.eye(c, dtype=F32) + lower
    rhs = jnp.concatenate([v * beta[..., None], kb * jnp.exp(gc)[..., None]], axis=-1)
    sol = lax.linalg.triangular_solve(a_mat, rhs, left_side=True, lower=True, unit_diagonal=True)
    u, w = sol[..., :dv], sol[..., dv:]
    qk = jnp.einsum('bhnid,bhnjd->bhnij', q, k) * dec_incl
    q_dec = q * jnp.exp(gc)[..., None]
    k_dec = k * jnp.exp(gc[..., -1:] - gc)[..., None]
    g_last = jnp.exp(gc[..., -1])

    def step(s, xs):
        u_n, w_n, qk_n, qd_n, kd_n, gl_n = xs
        v_new = u_n - jnp.einsum('bhcd,bhde->bhce', w_n, s)
        o = jnp.einsum('bhcd,bhde->bhce', qd_n, s) + jnp.einsum('bhij,bhje->bhie', qk_n, v_new)
        s = s * gl_n[..., None, None] + jnp.einsum('bhcd,bhce->bhde', kd_n, v_new)
        return s, o

    xs = tuple(jnp.moveaxis(a, 2, 0) for a in (u, w, qk, q_dec, k_dec, g_last))
    s_final, o = lax.scan(step, s0, xs)
    o = jnp.transpose(o, (1, 0, 3, 2, 4)).reshape(b, n * c, h, dv)[:, :t]
    return o, s_final


def gated_deltanet(q_raw, k_raw, v_raw, a, b_logit, z, conv_buf, s0, conv_w, a_log, dt_bias, out_norm):
    bsz, t, _ = q_raw.shape
    qkv, new_buf = causal_short_conv(jnp.concatenate([q_raw, k_raw, v_raw], axis=-1), conv_buf, conv_w)
    qkv = qkv.astype(F32).reshape(bsz, t, 3, GDN_HEADS, HEAD_DIM)
    q = l2norm(qkv[:, :, 0]) * (HEAD_DIM ** -0.5)
    k = l2norm(qkv[:, :, 1])
    v = qkv[:, :, 2]
    beta = jax.nn.sigmoid(b_logit.astype(F32))
    g = -jnp.exp(a_log.astype(F32)) * jax.nn.softplus(a.astype(F32) + dt_bias.astype(F32))
    o, s_new = gdn_chunked(q, k, v, g, beta, s0.astype(F32))
    o = rmsnorm(o, out_norm) * jax.nn.silu(z.astype(F32).reshape(bsz, t, GDN_HEADS, HEAD_DIM))
    return o.reshape(bsz, t, GDN_W).astype(q_raw.dtype), new_buf, s_new


def memory_kv(mem, g, w_mem_kv):
    bsz, m, _ = mem.shape
    kv = rmsnorm(mem, g) @ w_mem_kv
    k, v = jnp.split(kv, 2, axis=-1)
    return k.reshape(bsz, m, X_HEADS, HEAD_DIM), v.reshape(bsz, m, X_HEADS, HEAD_DIM)


def memory_attention(q, mem_k, mem_v):
    z = jnp.einsum('bqhd,bmhd->bhqm', q, mem_k, preferred_element_type=F32) / math.sqrt(HEAD_DIM)
    p = jax.nn.softmax(z, axis=-1)
    return jnp.einsum('bhqm,bmhd->bqhd', p, mem_v, preferred_element_type=F32)


def token_mixer(h, mem_k, mem_v, conv_buf, s0, past_k, past_v, lw):
    bsz, t, _ = h.shape
    proj = h @ lw['w_in']
    split_at = np.cumsum(IN_SPLITS)[:-1].tolist()
    (sb_q, sb_k, sb_v, g_q, g_k, g_v, g_a, g_b, g_z, x_q,
     gate_sb, gate_gdn, gate_x) = jnp.split(proj, split_at, axis=-1)
    heads = lambda a: a.reshape(bsz, t, -1, HEAD_DIM)
    sb_q, sb_k, sb_v = heads(sb_q), heads(sb_k), heads(sb_v)
    if past_k is None:
        o_sb = stick_breaking_prompt(sb_q, sb_k, sb_v, lw['sb_logit_bias'])
    else:
        o_sb = stick_breaking_sample(sb_q, sb_k, sb_v, past_k, past_v, lw['sb_logit_bias'])
    o_gdn, new_buf, new_s = gated_deltanet(g_q, g_k, g_v, g_a, g_b, g_z, conv_buf, s0, lw['gdn_conv_w'],
                                           lw['gdn_a_log'], lw['gdn_dt_bias'], lw['gdn_out_norm'])
    o_x = memory_attention(heads(x_q), mem_k, mem_v)
    merged = (jax.nn.sigmoid(gate_sb) * (o_sb.reshape(bsz, t, SB_W).astype(h.dtype) @ lw['w_up_sb'])
              + jax.nn.sigmoid(gate_gdn) * (o_gdn @ lw['w_up_gdn'])
              + jax.nn.sigmoid(gate_x) * (o_x.reshape(bsz, t, X_W).astype(h.dtype) @ lw['w_up_x']))
    return merged @ lw['w_out'], (sb_k, sb_v, new_buf, new_s)


def decoder_layer(x, mem_k, mem_v, conv_buf, s0, past_k, past_v, lw):
    x = x + 0.5 * swiglu(rmsnorm(x, lw['ffn1_norm']), lw['ffn1_w_in'], lw['ffn1_w_out'])
    mix, new_state = token_mixer(rmsnorm(x, lw['mix_norm']), mem_k, mem_v, conv_buf, s0, past_k, past_v, lw)
    x = x + mix
    x = x + 0.5 * swiglu(rmsnorm(x, lw['ffn2_norm']), lw['ffn2_w_in'], lw['ffn2_w_out'])
    return x, new_state


def setup_inputs(seed: int = 0) -> dict:
    key = jax.random.key(seed)
    ks = iter(jax.random.split(key, 40))
    nrm = lambda shape, scale: scale * jax.random.normal(next(ks), shape, F32)
    gain = lambda shape: 1.0 + nrm(shape, 0.02)
    n_pages = PAST_LEN // PAGE_SIZE
    n_phys = (DEC_BATCH * n_pages * 5) // 4
    page_table = jax.random.permutation(next(ks), n_phys)[:DEC_BATCH * n_pages]
    page_table = page_table.reshape(DEC_BATCH, n_pages).astype(jnp.int32)
    dt = jax.random.uniform(next(ks), (DEPTH, GDN_HEADS), F32, 0.001, 0.1)
    dt_bias = dt + jnp.log(-jnp.expm1(-dt))
    a_log = jnp.log(jax.random.uniform(next(ks), (DEPTH, GDN_HEADS), F32, 1.0, 16.0))
    return {
        'x_prompt': nrm((BATCH, SEQ, D_MODEL), 1.0),
        'x_sample': nrm((DEC_BATCH, DEC_SEQ, D_MODEL), 1.0),
        'cache_sb_k': nrm((DEPTH, n_phys, PAGE_SIZE, SB_HEADS, HEAD_DIM), 1.0),
        'cache_sb_v': nrm((DEPTH, n_phys, PAGE_SIZE, SB_HEADS, HEAD_DIM), 1.0),
        'cache_mem_k': nrm((DEPTH, DEC_BATCH, N_MEM, X_HEADS, HEAD_DIM), 1.0),
        'cache_mem_v': nrm((DEPTH, DEC_BATCH, N_MEM, X_HEADS, HEAD_DIM), 1.0),
        'state_gdn_S': nrm((DEPTH, DEC_BATCH, GDN_HEADS, HEAD_DIM, HEAD_DIM), 0.1),
        'state_gdn_conv': nrm((DEPTH, DEC_BATCH, CONV_K - 1, GDN_CONV_W), 1.0),
        'page_table': page_table,
        'mem_prompt': nrm((BATCH, N_MEM, D_MODEL), 1.0),
        'ffn1_norm': gain((DEPTH, D_MODEL)),
        'ffn1_w_in': nrm((DEPTH, D_MODEL, 2 * D_FF), D_MODEL ** -0.5),
        'ffn1_w_out': nrm((DEPTH, D_FF, D_MODEL), D_FF ** -0.5),
        'mix_norm': gain((DEPTH, D_MODEL)),
        'w_in': nrm((DEPTH, D_MODEL, IN_W), D_MODEL ** -0.5),
        'sb_logit_bias': SB_BIAS_INIT + nrm((DEPTH, SB_HEADS), 0.5),
        'gdn_conv_w': nrm((DEPTH, CONV_K, GDN_CONV_W), CONV_K ** -0.5),
        'gdn_a_log': a_log,
        'gdn_dt_bias': dt_bias,
        'gdn_out_norm': gain((DEPTH, HEAD_DIM)),
        'mem_norm': gain((DEPTH, D_MODEL)),
        'w_mem_kv': nrm((DEPTH, D_MODEL, 2 * X_W), D_MODEL ** -0.5),
        'w_up_sb': nrm((DEPTH, SB_W, D_MODEL), SB_W ** -0.5),
        'w_up_gdn': nrm((DEPTH, GDN_W, D_MODEL), GDN_W ** -0.5),
        'w_up_x': nrm((DEPTH, X_W, D_MODEL), X_W ** -0.5),
        'w_out': nrm((DEPTH, D_MODEL, D_MODEL), D_MODEL ** -0.5),
        'ffn2_norm': gain((DEPTH, D_MODEL)),
        'ffn2_w_in': nrm((DEPTH, D_MODEL, 2 * D_FF), D_MODEL ** -0.5),
        'ffn2_w_out': nrm((DEPTH, D_FF, D_MODEL), D_FF ** -0.5),
        'final_norm': gain((D_MODEL,)),
    }


def reference(x_prompt, x_sample, cache_sb_k, cache_sb_v, cache_mem_k, cache_mem_v, state_gdn_S,
              state_gdn_conv, page_table, mem_prompt, ffn1_norm, ffn1_w_in, ffn1_w_out, mix_norm, w_in,
              sb_logit_bias, gdn_conv_w, gdn_a_log, gdn_dt_bias, gdn_out_norm, mem_norm, w_mem_kv, w_up_sb,
              w_up_gdn, w_up_x, w_out, ffn2_norm, ffn2_w_in, ffn2_w_out, final_norm):
    n_pages = PAST_LEN // PAGE_SIZE
    xp, xs = x_prompt, x_sample
    kp_l, vp_l, mkp_l, mvp_l, sp_l, cp_l = [], [], [], [], [], []
    ks_l, vs_l, ss_l, cs_l = [], [], [], []
    for l in range(DEPTH):
        lw = {'ffn1_norm': ffn1_norm[l], 'ffn1_w_in': ffn1_w_in[l], 'ffn1_w_out': ffn1_w_out[l],
              'mix_norm': mix_norm[l], 'w_in': w_in[l], 'sb_logit_bias': sb_logit_bias[l],
              'gdn_conv_w': gdn_conv_w[l],
              'gdn_a_log': gdn_a_log[l], 'gdn_dt_bias': gdn_dt_bias[l], 'gdn_out_norm': gdn_out_norm[l],
              'w_up_sb': w_up_sb[l], 'w_up_gdn': w_up_gdn[l], 'w_up_x': w_up_x[l], 'w_out': w_out[l],
              'ffn2_norm': ffn2_norm[l], 'ffn2_w_in': ffn2_w_in[l], 'ffn2_w_out': ffn2_w_out[l]}
        mk_p, mv_p = memory_kv(mem_prompt, mem_norm[l], w_mem_kv[l])
        conv0 = jnp.zeros((xp.shape[0], CONV_K - 1, GDN_CONV_W), xp.dtype)
        s0 = jnp.zeros((xp.shape[0], GDN_HEADS, HEAD_DIM, HEAD_DIM), F32)
        xp, (k_p, v_p, c_p, s_p) = decoder_layer(xp, mk_p, mv_p, conv0, s0, None, None, lw)
        past_k = cache_sb_k[l][page_table].reshape(DEC_BATCH, n_pages * PAGE_SIZE, SB_HEADS, HEAD_DIM)
        past_v = cache_sb_v[l][page_table].reshape(DEC_BATCH, n_pages * PAGE_SIZE, SB_HEADS, HEAD_DIM)
        xs, (k_s, v_s, c_s, s_s) = decoder_layer(xs, cache_mem_k[l], cache_mem_v[l], state_gdn_conv[l],
                                                 state_gdn_S[l], past_k, past_v, lw)
        kp_l.append(k_p); vp_l.append(v_p); mkp_l.append(mk_p); mvp_l.append(mv_p)
        sp_l.append(s_p); cp_l.append(c_p)
        ks_l.append(k_s); vs_l.append(v_s); ss_l.append(s_s); cs_l.append(c_s)
    y_prompt = rmsnorm(xp, final_norm)
    y_sample = rmsnorm(xs, final_norm)
    return (y_prompt, y_sample, jnp.stack(kp_l), jnp.stack(vp_l), jnp.stack(mkp_l), jnp.stack(mvp_l),
            jnp.stack(sp_l), jnp.stack(cp_l), jnp.stack(ks_l), jnp.stack(vs_l), jnp.stack(ss_l),
            jnp.stack(cs_l))
```

```python
import functools
import math

import jax
import jax.numpy as jnp
from jax import lax
from jax.experimental import pallas as pl
from jax.experimental.pallas import tpu as pltpu

F32 = jnp.float32
BF16 = jnp.bfloat16
NORM_EPS = 1e-6
LANES = 128
SUBLANES = 8
GDN_CHUNK = 64
VMEM_LIMIT = 56 * 1024 * 1024
HIGHEST = lax.Precision.HIGHEST

_NT = (((1,), (1,)), ((), ()))


def _params(*sem):
    return pltpu.CompilerParams(dimension_semantics=sem, vmem_limit_bytes=VMEM_LIMIT)


def _rmsnorm(x, g):
    return x * lax.rsqrt(jnp.mean(x * x, axis=-1, keepdims=True) + NORM_EPS) * g


def _softplus(x):
    return jnp.maximum(x, 0.0) + jnp.log(1.0 + jnp.exp(-jnp.abs(x)))


def _silu(x):
    return x * jax.nn.sigmoid(x)


def _pick(n, pref):
    if n <= pref:
        return n
    t = pref
    while n % t:
        t //= 2
    assert t >= SUBLANES, (n, pref)
    return t


def _ffn_kernel(x_ref, g_ref, wg_ref, wu_ref, wo_ref, fg_ref, o_ref, xn_ref, *, final):
    f = pl.program_id(1)

    @pl.when(f == 0)
    def _():
        xn_ref[...] = _rmsnorm(x_ref[...], g_ref[...]).astype(BF16)
        o_ref[...] = jnp.zeros_like(o_ref)

    xn = xn_ref[...]
    gate = jnp.dot(xn, wg_ref[...], preferred_element_type=F32)
    up = jnp.dot(xn, wu_ref[...], preferred_element_type=F32)
    h = (_silu(gate) * up).astype(BF16)
    o_ref[...] += jnp.dot(h, wo_ref[...], preferred_element_type=F32)

    @pl.when(f == pl.num_programs(1) - 1)
    def _():
        y = x_ref[...] + 0.5 * o_ref[...]
        if final:
            y = _rmsnorm(y, fg_ref[...])
        o_ref[...] = y


def _ffn(x, g, w_in, w_out, final_g, *, final):
    t, d = x.shape
    dff = w_out.shape[0]
    tm = _pick(t, 512)
    tf = _pick(dff, 512)
    nf = dff // tf
    return pl.pallas_call(
        functools.partial(_ffn_kernel, final=final),
        grid=(t // tm, nf),
        in_specs=[
            pl.BlockSpec((tm, d), lambda i, f: (i, 0)),
            pl.BlockSpec((1, d), lambda i, f: (0, 0)),
            pl.BlockSpec((d, tf), lambda i, f: (0, f)),
            pl.BlockSpec((d, tf), lambda i, f: (0, f + nf)),
            pl.BlockSpec((tf, d), lambda i, f: (f, 0)),
            pl.BlockSpec((1, d), lambda i, f: (0, 0)),
        ],
        out_specs=pl.BlockSpec((tm, d), lambda i, f: (i, 0)),
        out_shape=jax.ShapeDtypeStruct((t, d), F32),
        scratch_shapes=[pltpu.VMEM((tm, d), BF16)],
        compiler_params=_params("parallel", "arbitrary"),
        name="ffn_final" if final else "ffn",
    )(x, g.reshape(1, d), w_in, w_in, w_out, final_g.reshape(1, d))


def _nmm_kernel(x_ref, g_ref, w_ref, o_ref, xn_ref):
    @pl.when(pl.program_id(1) == 0)
    def _():
        xn_ref[...] = _rmsnorm(x_ref[...], g_ref[...]).astype(BF16)

    o_ref[...] = jnp.dot(xn_ref[...], w_ref[...], preferred_element_type=F32)


def _normed_matmul(x, g, w, *, name):
    t, d = x.shape
    n = w.shape[1]
    tm = _pick(t, 512)
    tn = _pick(n, 2048)
    return pl.pallas_call(
        _nmm_kernel,
        grid=(t // tm, n // tn),
        in_specs=[
            pl.BlockSpec((tm, d), lambda i, j: (i, 0)),
            pl.BlockSpec((1, d), lambda i, j: (0, 0)),
            pl.BlockSpec((d, tn), lambda i, j: (0, j)),
        ],
        out_specs=pl.BlockSpec((tm, tn), lambda i, j: (i, j)),
        out_shape=jax.ShapeDtypeStruct((t, n), F32),
        scratch_shapes=[pltpu.VMEM((tm, d), BF16)],
        compiler_params=_params("parallel", "arbitrary"),
        name=name,
    )(x, g.reshape(1, d), w)


def _sb_block(z, valid, u, carry, v_bf):
    w_keys = z.shape[1]
    lr = -_softplus(z)
    if valid is not None:
        lr = jnp.where(valid, lr, 0.0)
    hi = lr.astype(BF16)
    lo = (lr - hi.astype(F32)).astype(BF16)
    rc = (jnp.dot(hi, u, preferred_element_type=F32)
          + jnp.dot(lo, u, preferred_element_type=F32))
    e = z + rc[:, :w_keys] + jnp.concatenate([carry] * (w_keys // LANES), axis=1)
    w = jnp.exp(e)
    if valid is not None:
        w = jnp.where(valid, w, 0.0)
    pv = jnp.dot(w.astype(BF16), v_bf, preferred_element_type=F32)
    return pv, carry + rc[:, w_keys:]


def _suffix_matrix(w_keys):
    j = lax.broadcasted_iota(jnp.int32, (w_keys, w_keys + LANES), 0)
    s = lax.broadcasted_iota(jnp.int32, (w_keys, w_keys + LANES), 1)
    return ((j >= s) | (s >= w_keys)).astype(BF16)


def _sb_prompt_kernel(bias_ref, q_ref, k_ref, v_ref, u_ref, o_ref, kbf, vbf, acc, car, *, scale, blk):
    h = pl.program_id(1)
    i = pl.program_id(2)

    @pl.when(i == 0)
    def _():
        kbf[...] = k_ref[...].astype(BF16)
        vbf[...] = v_ref[...].astype(BF16)

    q = (q_ref[...] * scale).astype(BF16)
    bias = bias_ref[h]
    u = u_ref[...]
    rows = lax.broadcasted_iota(jnp.int32, (blk, blk), 0)
    cols = lax.broadcasted_iota(jnp.int32, (blk, blk), 1)

    def block(j, valid):
        off = pl.multiple_of(j * blk, blk)
        z = lax.dot_general(q, kbf[pl.ds(off, blk), :], _NT, preferred_element_type=F32) + bias
        return _sb_block(z, valid, u, car[...], vbf[pl.ds(off, blk), :])

    car[...] = jnp.zeros_like(car)
    pv, c = block(i, cols < rows)
    acc[...] = pv
    car[...] = c

    def body(n, _):
        pv, c = block(i - 1 - n, None)
        acc[...] += pv
        car[...] = c
        return 0

    lax.fori_loop(0, i, body, 0)
    o_ref[...] = acc[...].astype(o_ref.dtype)


def _sb_prompt(proj, bias, *, batch, seq, heads, k_col, v_col):
    blk = _pick(seq, 256)
    assert blk % LANES == 0 and seq % blk == 0
    nq = seq // blk
    return pl.pallas_call(
        functools.partial(_sb_prompt_kernel, scale=1.0 / math.sqrt(LANES), blk=blk),
        grid=(batch, heads, nq),
        in_specs=[
            pl.BlockSpec(memory_space=pltpu.SMEM),
            pl.BlockSpec((blk, LANES), lambda b, h, i: (b * nq + i, h)),
            pl.BlockSpec((seq, LANES), lambda b, h, i: (b, k_col + h)),
            pl.BlockSpec((seq, LANES), lambda b, h, i: (b, v_col + h)),
            pl.BlockSpec((blk, blk + LANES), lambda b, h, i: (0, 0)),
        ],
        out_specs=pl.BlockSpec((blk, LANES), lambda b, h, i: (b * nq + i, h)),
        out_shape=jax.ShapeDtypeStruct((batch * seq, heads * LANES), BF16),
        scratch_shapes=[
            pltpu.VMEM((seq, LANES), BF16),
            pltpu.VMEM((seq, LANES), BF16),
            pltpu.VMEM((blk, LANES), F32),
            pltpu.VMEM((blk, LANES), F32),
        ],
        compiler_params=_params("parallel", "parallel", "arbitrary"),
        name="sb_prompt",
    )(bias, proj, proj, proj, _suffix_matrix(blk))


def _sb_sample_kernel(pt_ref, q_ref, bias_ref, tpos_ref, u_ref, kn_ref, vn_ref, *rest, pages):
    k_refs = rest[:pages]
    v_refs = rest[pages:2 * pages]
    o_ref, acc, car = rest[2 * pages:]
    s = pl.program_id(1)
    q = q_ref[...]
    bias = bias_ref[...]
    u = u_ref[...]

    def page(k, v, valid):
        z = lax.dot_general(q, k.astype(BF16), _NT, preferred_element_type=F32) + bias
        pv, c = _sb_block(z, valid, u, car[...], v.astype(BF16))
        acc[...] += pv
        car[...] = c

    @pl.when(s == 0)
    def _():
        acc[...] = jnp.zeros_like(acc)
        car[...] = jnp.zeros_like(car)
        key = lax.broadcasted_iota(jnp.int32, tpos_ref.shape, 1)
        page(kn_ref[...], vn_ref[...], key < tpos_ref[...])

    for p in range(pages):
        page(k_refs[p][...], v_refs[p][...], None)

    @pl.when(s == pl.num_programs(1) - 1)
    def _():
        o_ref[...] = acc[...]


def _sb_sample(q, k_new, v_new, cache_k, cache_v, page_table, bias):
    b, t, h, _ = q.shape
    n_phys, page = cache_k.shape[:2]
    n_pages = page_table.shape[1]
    assert page == LANES
    width = h * LANES
    rows = -(-h * t // SUBLANES) * SUBLANES
    pages = _pick(n_pages, 8)
    scale = 1.0 / math.sqrt(LANES)
    eye = jnp.eye(h, dtype=F32)
    q_rows = jnp.einsum("bthd,hg->bhtgd", q * scale, eye).reshape(b, h * t, width)
    q_rows = jnp.pad(q_rows, ((0, 0), (0, rows - h * t), (0, 0))).astype(BF16)
    bias_rows = jnp.pad(jnp.repeat(bias.astype(F32), t), (0, rows - h * t))
    bias_rows = jnp.broadcast_to(bias_rows[:, None], (rows, LANES))
    tpos = jnp.pad(jnp.tile(jnp.arange(t, dtype=jnp.int32), h), (0, rows - h * t))
    tpos = jnp.broadcast_to(tpos[:, None], (rows, LANES))
    pad_new = lambda a: jnp.pad(a.reshape(b, t, width), ((0, 0), (0, page - t), (0, 0)))
    ck = cache_k.reshape(n_phys, page, width)
    cv = cache_v.reshape(n_phys, page, width)

    def page_spec(p):
        return pl.BlockSpec((None, page, width),
                            lambda i, s, pt: (pt[i, n_pages - 1 - (s * pages + p)], 0, 0))

    const = lambda shape: pl.BlockSpec(shape, lambda i, s, pt: (0,) * len(shape))
    per_b = lambda r, c: pl.BlockSpec((None, r, c), lambda i, s, pt: (i, 0, 0))
    o_rows = pl.pallas_call(
        functools.partial(_sb_sample_kernel, pages=pages),
        grid_spec=pltpu.PrefetchScalarGridSpec(
            num_scalar_prefetch=1,
            grid=(b, n_pages // pages),
            in_specs=[per_b(rows, width), const((rows, LANES)), const((rows, LANES)),
                      const((page, page + LANES)), per_b(page, width), per_b(page, width)]
                     + [page_spec(p) for p in range(pages)] * 2,
            out_specs=per_b(rows, width),
            scratch_shapes=[pltpu.VMEM((rows, width), F32), pltpu.VMEM((rows, LANES), F32)],
        ),
        out_shape=jax.ShapeDtypeStruct((b, rows, width), F32),
        compiler_params=_params("parallel", "arbitrary"),
        name="sb_sample",
    )(page_table, q_rows, bias_rows, tpos, _suffix_matrix(page), pad_new(k_new), pad_new(v_new),
      *([ck] * pages), *([cv] * pages))
    o = o_rows[:, :h * t].reshape(b, h, t, h, LANES)
    return jnp.einsum("bhtgd,hg->bthd", o, eye)


def _dot_hi(a, b):
    return jnp.dot(a, b, precision=HIGHEST, preferred_element_type=F32)


def _dot_nt_hi(a, b):
    return lax.dot_general(a, b, _NT, precision=HIGHEST, preferred_element_type=F32)


def _gdn_kernel(raw_ref, prev_ref, buf_ref, a_ref, b_ref, z_ref, s0_ref, cw_ref, alog_ref, dtb_ref,
                on_ref, o_ref, s_ref, xfull, *, heads, conv_k, valid_last):
    c = pl.program_id(1)
    chunk = raw_ref.shape[0]
    gw = heads * LANES

    @pl.when(c == 0)
    def _():
        s_ref[...] = s0_ref[...]
        xfull[0:SUBLANES, :] = buf_ref[...]

    @pl.when(c > 0)
    def _():
        xfull[0:SUBLANES, :] = prev_ref[...]

    xfull[SUBLANES:SUBLANES + chunk, :] = raw_ref[...]
    y = None
    for j in range(conv_k):
        start = SUBLANES - (conv_k - 1) + j
        term = xfull[start:start + chunk, :] * cw_ref[j:j + 1, :]
        y = term if y is None else y + term
    y = _silu(y)

    row = lax.broadcasted_iota(jnp.int32, (chunk, 1), 0)
    live = None
    if valid_last < chunk:
        live = row < jnp.where(c < pl.num_programs(1) - 1, chunk, valid_last)

    g = -jnp.exp(alog_ref[...]) * _softplus(a_ref[...] + dtb_ref[...])
    beta = jax.nn.sigmoid(b_ref[...])
    if live is not None:
        g = jnp.where(live, g, 0.0)
        beta = jnp.where(live, beta, 0.0)

    ri = lax.broadcasted_iota(jnp.int32, (chunk, chunk), 0)
    ci = lax.broadcasted_iota(jnp.int32, (chunk, chunk), 1)
    incl = ri >= ci
    strict = ri > ci
    eye_c = (ri == ci).astype(F32)
    gc = _dot_hi(incl.astype(F32), g)
    sel = (lax.broadcasted_iota(jnp.int32, (SUBLANES, LANES), 0)
           == lax.broadcasted_iota(jnp.int32, (SUBLANES, LANES), 1)).astype(F32)
    gc_rows = _dot_nt_hi(sel, gc)

    for h in range(heads):
        sl = slice(h * LANES, (h + 1) * LANES)
        qh = y[:, h * LANES:(h + 1) * LANES]
        kh = y[:, gw + h * LANES:gw + (h + 1) * LANES]
        vh = y[:, 2 * gw + h * LANES:2 * gw + (h + 1) * LANES]
        qh = qh * lax.rsqrt(jnp.sum(qh * qh, axis=-1, keepdims=True) + NORM_EPS) * (LANES ** -0.5)
        kh = kh * lax.rsqrt(jnp.sum(kh * kh, axis=-1, keepdims=True) + NORM_EPS)
        if live is not None:
            qh = jnp.where(live, qh, 0.0)
            kh = jnp.where(live, kh, 0.0)
            vh = jnp.where(live, vh, 0.0)
        g_col = gc[:, h:h + 1]
        g_row = gc_rows[h:h + 1, :]
        g_last = gc[chunk - 1:chunk, h:h + 1]
        decay = jnp.exp(jnp.where(incl, g_col - g_row, 0.0))
        b_col = beta[:, h:h + 1]
        eg = jnp.exp(g_col)
        kb = kh * b_col
        low = _dot_nt_hi(kb, kh) * jnp.where(strict, decay, 0.0)
        inv = eye_c - low
        pw = low
        for _ in range(int(math.log2(chunk)) - 1):
            pw = _dot_hi(pw, pw)
            inv = inv + _dot_hi(inv, pw)
        sol = _dot_hi(inv, jnp.concatenate([vh * b_col, kb * eg], axis=1))
        u_h = sol[:, :LANES]
        w_h = sol[:, LANES:]
        qk = _dot_nt_hi(qh, kh) * jnp.where(incl, decay, 0.0)
        s_old = s_ref[h]
        v_new = u_h - _dot_hi(w_h, s_old)
        o_h = _dot_hi(qh * eg, s_old) + _dot_hi(qk, v_new)
        k_dec = kh * jnp.exp(g_last - g_col)
        s_ref[h] = s_old * jnp.exp(g_last) + _dot_hi(k_dec.T, v_new)
        o_h = _rmsnorm(o_h, on_ref[...]) * _silu(z_ref[:, sl])
        o_ref[:, sl] = o_h.astype(o_ref.dtype)


def _gdn(raw, a, b, z, conv_buf, s0, conv_w, a_log, dt_bias, out_norm, *, batch, seq, valid_last):
    heads = s0.shape[1]
    gw = heads * LANES
    conv_k = conv_w.shape[0]
    chunk = GDN_CHUNK
    assert seq % chunk == 0 and heads <= SUBLANES and conv_k - 1 <= SUBLANES
    nc = seq // chunk
    per_sub = chunk // SUBLANES
    buf = jnp.pad(conv_buf, ((0, 0), (SUBLANES - (conv_k - 1), 0), (0, 0)))
    lane_vec = lambda v: jnp.pad(v.astype(F32), (0, LANES - heads)).reshape(1, LANES)
    (raw_a, raw_c), (a_a, a_c), (b_a, b_c), (z_a, z_c) = raw, a, b, z
    tok = lambda width, col: pl.BlockSpec((chunk, width), lambda i, c: (i * nc + c, col))
    const = lambda shape: pl.BlockSpec(shape, lambda i, c: (0,) * len(shape))
    state = pl.BlockSpec((None, heads, LANES, LANES), lambda i, c: (i, 0, 0, 0))
    return pl.pallas_call(
        functools.partial(_gdn_kernel, heads=heads, conv_k=conv_k, valid_last=valid_last),
        grid=(batch, nc),
        in_specs=[
            tok(3 * gw, raw_c),
            pl.BlockSpec((SUBLANES, 3 * gw),
                         lambda i, c: (jnp.maximum((i * nc + c) * per_sub - 1, 0), raw_c)),
            pl.BlockSpec((None, SUBLANES, 3 * gw), lambda i, c: (i, 0, 0)),
            tok(LANES, a_c), tok(LANES, b_c), tok(gw, z_c),
            state,
            const((conv_k, 3 * gw)), const((1, LANES)), const((1, LANES)), const((1, LANES)),
        ],
        out_specs=[tok(gw, 0), state],
        out_shape=[jax.ShapeDtypeStruct((batch * seq, gw), BF16),
                   jax.ShapeDtypeStruct(s0.shape, F32)],
        scratch_shapes=[pltpu.VMEM((SUBLANES + chunk, 3 * gw), F32)],
        compiler_params=_params("parallel", "arbitrary"),
        name="gdn",
    )(raw_a, raw_a, buf, a_a, b_a, z_a, s0.astype(F32), conv_w, lane_vec(a_log), lane_vec(dt_bias),
      out_norm.astype(F32).reshape(1, LANES))


def _mem_attn_kernel(q_ref, k_ref, v_ref, o_ref, *, heads, scale):
    for h in range(heads):
        sl = slice(h * LANES, (h + 1) * LANES)
        q = (q_ref[:, sl] * scale).astype(BF16)
        z = lax.dot_general(q, k_ref[:, sl].astype(BF16), _NT, preferred_element_type=F32)
        e = jnp.exp(z - jnp.max(z, axis=-1, keepdims=True))
        p = e / jnp.sum(e, axis=-1, keepdims=True)
        o = jnp.dot(p.astype(BF16), v_ref[:, sl].astype(BF16), preferred_element_type=F32)
        o_ref[:, sl] = o.astype(o_ref.dtype)


def _mem_attn(q, mem_k, mem_v, *, batch, seq):
    q_a, q_c = q
    n_mem, xw = mem_k.shape[1:]
    tq = _pick(seq, 512)
    nq = seq // tq
    mem = pl.BlockSpec((None, n_mem, xw), lambda i, j: (i, 0, 0))
    return pl.pallas_call(
        functools.partial(_mem_attn_kernel, heads=xw // LANES, scale=1.0 / math.sqrt(LANES)),
        grid=(batch, nq),
        in_specs=[pl.BlockSpec((tq, xw), lambda i, j: (i * nq + j, q_c)), mem, mem],
        out_specs=pl.BlockSpec((tq, xw), lambda i, j: (i * nq + j, 0)),
        out_shape=jax.ShapeDtypeStruct((batch * seq, xw), BF16),
        compiler_params=_params("parallel", "parallel"),
        name="mem_attn",
    )(q_a, mem_k, mem_v)


def _merge_kernel(x_ref, osb_ref, ogdn_ref, ox_ref, gsb_ref, ggdn_ref, gx_ref,
                  wsb_ref, wgdn_ref, wx_ref, wout_ref, o_ref):
    up = lambda o, w: jnp.dot(o[...], w[...], preferred_element_type=F32)
    merged = (jax.nn.sigmoid(gsb_ref[...]) * up(osb_ref, wsb_ref)
              + jax.nn.sigmoid(ggdn_ref[...]) * up(ogdn_ref, wgdn_ref)
              + jax.nn.sigmoid(gx_ref[...]) * up(ox_ref, wx_ref))
    o_ref[...] = x_ref[...] + jnp.dot(merged.astype(BF16), wout_ref[...], preferred_element_type=F32)


def _merge(x, o_sb, o_gdn, o_x, proj, gate_col, w_up_sb, w_up_gdn, w_up_x, w_out):
    t, d = x.shape
    tm = _pick(t, 256)
    tok = lambda a: pl.BlockSpec((tm, a.shape[1]), lambda i: (i, 0))
    gate = lambda n: pl.BlockSpec((tm, d), lambda i: (i, gate_col + n))
    weight = lambda w: pl.BlockSpec(w.shape, lambda i: (0, 0), pipeline_mode=pl.Buffered(1))
    return pl.pallas_call(
        _merge_kernel,
        grid=(t // tm,),
        in_specs=[tok(x), tok(o_sb), tok(o_gdn), tok(o_x), gate(0), gate(1), gate(2),
                  weight(w_up_sb), weight(w_up_gdn), weight(w_up_x), weight(w_out)],
        out_specs=tok(x),
        out_shape=jax.ShapeDtypeStruct((t, d), F32),
        compiler_params=_params("parallel"),
        name="merge",
    )(x, o_sb, o_gdn, o_x, proj, proj, proj, w_up_sb, w_up_gdn, w_up_x, w_out)


def _pack_w_in(w_in, sbw, gw, xw, d, g_heads):
    o = 3 * sbw + 3 * gw
    qkv = w_in[:, :o]
    w_a = w_in[:, o:o + g_heads]
    w_b = w_in[:, o + g_heads:o + 2 * g_heads]
    o += 2 * g_heads
    w_z = w_in[:, o:o + gw]
    w_xq = w_in[:, o + gw:o + gw + xw]
    gates = w_in[:, o + gw + xw:]
    assert gates.shape[1] == 3 * d
    pad = lambda w: jnp.pad(w, ((0, 0), (0, LANES - g_heads)))
    return jnp.concatenate([qkv, w_z, pad(w_a), pad(w_b), w_xq, gates], axis=1).astype(BF16)


def kernel(x_prompt, x_sample, cache_sb_k, cache_sb_v, cache_mem_k, cache_mem_v, state_gdn_S, state_gdn_conv, page_table, mem_prompt, ffn1_norm, ffn1_w_in, ffn1_w_out, mix_norm, w_in, sb_logit_bias, gdn_conv_w, gdn_a_log, gdn_dt_bias, gdn_out_norm, mem_norm, w_mem_kv, w_up_sb, w_up_gdn, w_up_x, w_out, ffn2_norm, ffn2_w_in, ffn2_w_out, final_norm):
    batch, seq, d = x_prompt.shape
    dec_batch, dec_seq, _ = x_sample.shape
    depth = w_in.shape[0]
    sb_heads = cache_sb_k.shape[3]
    x_heads = cache_mem_k.shape[3]
    g_heads = state_gdn_S.shape[2]
    n_mem = mem_prompt.shape[1]
    conv_k = gdn_conv_w.shape[1]
    sbw, gw, xw = sb_heads * LANES, g_heads * LANES, x_heads * LANES
    assert sbw == gw, "column-block addressing below assumes equal stick-breaking and DeltaNet widths"

    off_gdn, off_z = 3 * sbw, 3 * sbw + 3 * gw
    off_a, off_b, off_xq = off_z + gw, off_z + gw + LANES, off_z + gw + 2 * LANES
    off_gate = off_xq + xw
    n_proj = off_gate + 3 * d
    assert off_gdn % (3 * gw) == 0 and off_gate % d == 0 and off_z % gw == 0 and off_xq % xw == 0
    assert n_proj % LANES == 0

    xp = x_prompt.reshape(batch * seq, d)
    xs = x_sample.reshape(dec_batch * dec_seq, d)
    outs = [[] for _ in range(10)]
    for l in range(depth):
        bf = lambda w: w[l].astype(BF16)
        w1_in, w1_out, w2_in, w2_out = bf(ffn1_w_in), bf(ffn1_w_out), bf(ffn2_w_in), bf(ffn2_w_out)
        wp = _pack_w_in(w_in[l], sbw, gw, xw, d, g_heads)
        wup_sb, wup_gdn, wup_x, wo = bf(w_up_sb), bf(w_up_gdn), bf(w_up_x), bf(w_out)
        last = l == depth - 1
        mixer_w = (gdn_conv_w[l], gdn_a_log[l], gdn_dt_bias[l], gdn_out_norm[l])

        def mix_tail(x, proj, o_sb, o_gdn, o_x):
            x = _merge(x, o_sb, o_gdn, o_x, proj, off_gate // d, wup_sb, wup_gdn, wup_x, wo)
            return _ffn(x, ffn2_norm[l], w2_in, w2_out, final_norm, final=last)

        mem_kv = _normed_matmul(mem_prompt.reshape(batch * n_mem, d), mem_norm[l], bf(w_mem_kv),
                                name="mem_kv")
        mk_p = mem_kv[:, :xw].reshape(batch, n_mem, xw)
        mv_p = mem_kv[:, xw:].reshape(batch, n_mem, xw)
        xp = _ffn(xp, ffn1_norm[l], w1_in, w1_out, final_norm, final=False)
        proj = _normed_matmul(xp, mix_norm[l], wp, name="mixer_proj")
        o_sb = _sb_prompt(proj, sb_logit_bias[l].astype(F32), batch=batch, seq=seq, heads=sb_heads,
                          k_col=sb_heads, v_col=2 * sb_heads)
        o_gdn, s_p = _gdn((proj, off_gdn // (3 * gw)), (proj, off_a // LANES), (proj, off_b // LANES),
                          (proj, off_z // gw), jnp.zeros((batch, conv_k - 1, 3 * gw), F32),
                          jnp.zeros((batch, g_heads, LANES, LANES), F32), *mixer_w,
                          batch=batch, seq=seq, valid_last=GDN_CHUNK)
        o_x = _mem_attn((proj, off_xq // xw), mk_p, mv_p, batch=batch, seq=seq)
        xp = mix_tail(xp, proj, o_sb, o_gdn, o_x)
        proj3 = proj.reshape(batch, seq, n_proj)
        outs[0].append(proj3[:, :, sbw:2 * sbw].reshape(batch, seq, sb_heads, LANES))
        outs[1].append(proj3[:, :, 2 * sbw:3 * sbw].reshape(batch, seq, sb_heads, LANES))
        outs[2].append(mk_p.reshape(batch, n_mem, x_heads, LANES))
        outs[3].append(mv_p.reshape(batch, n_mem, x_heads, LANES))
        outs[4].append(s_p)
        assert seq >= conv_k - 1
        outs[5].append(proj3[:, seq - (conv_k - 1):, off_gdn:off_gdn + 3 * gw])

        xs = _ffn(xs, ffn1_norm[l], w1_in, w1_out, final_norm, final=False)
        proj = _normed_matmul(xs, mix_norm[l], wp, name="mixer_proj_sample")
        proj3 = proj.reshape(dec_batch, dec_seq, n_proj)
        heads4 = lambda a: a.reshape(dec_batch, dec_seq, sb_heads, LANES)
        q_s, k_s, v_s = (heads4(proj3[:, :, n * sbw:(n + 1) * sbw]) for n in range(3))
        o_sb = _sb_sample(q_s, k_s, v_s, cache_sb_k[l], cache_sb_v[l], page_table, sb_logit_bias[l])
        o_sb = o_sb.reshape(dec_batch * dec_seq, sbw).astype(BF16)
        seq_g = -(-dec_seq // GDN_CHUNK) * GDN_CHUNK
        seq_x = -(-dec_seq // SUBLANES) * SUBLANES
        padded = lambda lo, hi, n: jnp.pad(proj3[:, :, lo:hi], ((0, 0), (0, n - dec_seq), (0, 0))
                                           ).reshape(dec_batch * n, hi - lo)
        raw_s = proj3[:, :, off_gdn:off_gdn + 3 * gw]
        o_gdn, s_s = _gdn((padded(off_gdn, off_gdn + 3 * gw, seq_g), 0), (padded(off_a, off_a + LANES, seq_g), 0),
                          (padded(off_b, off_b + LANES, seq_g), 0), (padded(off_z, off_z + gw, seq_g), 0),
                          state_gdn_conv[l], state_gdn_S[l], *mixer_w,
                          batch=dec_batch, seq=seq_g, valid_last=dec_seq - (seq_g - GDN_CHUNK))
        o_gdn = o_gdn.reshape(dec_batch, seq_g, gw)[:, :dec_seq].reshape(dec_batch * dec_seq, gw)
        o_x = _mem_attn((padded(off_xq, off_xq + xw, seq_x), 0), cache_mem_k[l].reshape(dec_batch, n_mem, xw),
                        cache_mem_v[l].reshape(dec_batch, n_mem, xw), batch=dec_batch, seq=seq_x)
        o_x = o_x.reshape(dec_batch, seq_x, xw)[:, :dec_seq].reshape(dec_batch * dec_seq, xw)
        xs = mix_tail(xs, proj, o_sb, o_gdn, o_x)
        outs[6].append(k_s)
        outs[7].append(v_s)
        outs[8].append(s_s)
        conv_all = jnp.concatenate([state_gdn_conv[l].astype(F32), raw_s], axis=1)
        outs[9].append(conv_all[:, dec_seq:])

    stack = [jnp.stack(o) for o in outs]
    return (xp.reshape(batch, seq, d), xs.reshape(dec_batch, dec_seq, d), *stack)
```

```python
import functools
import math

import jax
import jax.numpy as jnp
from jax import lax
from jax.experimental import pallas as pl
from jax.experimental.pallas import tpu as pltpu

F32 = jnp.float32
BF16 = jnp.bfloat16
NORM_EPS = 1e-6
LANES = 128
SUBLANES = 8
GDN_CHUNK = 64
VMEM_LIMIT = 56 * 1024 * 1024
HIGHEST = lax.Precision.HIGHEST

_NT = (((1,), (1,)), ((), ()))
_TN = (((0,), (0,)), ((), ()))


def _params(*sem):
    return pltpu.CompilerParams(dimension_semantics=sem, vmem_limit_bytes=VMEM_LIMIT)


def _dot(a, b):
    return jnp.dot(a, b, preferred_element_type=F32)


def _rmsnorm(x, g):
    return x * lax.rsqrt(jnp.mean(x * x, axis=-1, keepdims=True) + NORM_EPS) * g


def _softplus(x):
    return jnp.maximum(x, 0.0) + jnp.log(1.0 + jnp.exp(-jnp.abs(x)))


def _silu(x):
    return x * jax.nn.sigmoid(x)


def _pick(n, pref):
    if n <= pref:
        return n
    t = pref
    while n % t:
        t //= 2
    assert t >= SUBLANES, (n, pref)
    return t


def _divisor(n, pref):
    return max(k for k in range(1, min(n, pref) + 1) if n % k == 0)


def _round_up(n, m):
    return -(-n // m) * m


def _ffn_kernel(x_ref, g_ref, wg_ref, wu_ref, wo_ref, fg_ref, o_ref, xn_ref, *, final):
    f = pl.program_id(1)

    @pl.when(f == 0)
    def _():
        xn_ref[...] = _rmsnorm(x_ref[...], g_ref[...]).astype(BF16)
        o_ref[...] = jnp.zeros_like(o_ref)

    xn = xn_ref[...]
    h = (_silu(_dot(xn, wg_ref[...])) * _dot(xn, wu_ref[...])).astype(BF16)
    o_ref[...] += _dot(h, wo_ref[...])

    @pl.when(f == pl.num_programs(1) - 1)
    def _():
        y = x_ref[...] + 0.5 * o_ref[...]
        if final:
            y = _rmsnorm(y, fg_ref[...])
        o_ref[...] = y


def _ffn(x, g, w_in, w_out, final_g, *, final):
    t, d = x.shape
    dff = w_out.shape[0]
    tm = _pick(t, 512)
    tf = _pick(dff, 512)
    nf = dff // tf
    return pl.pallas_call(
        functools.partial(_ffn_kernel, final=final),
        grid=(t // tm, nf),
        in_specs=[
            pl.BlockSpec((tm, d), lambda i, f: (i, 0)),
            pl.BlockSpec((1, d), lambda i, f: (0, 0)),
            pl.BlockSpec((d, tf), lambda i, f: (0, f)),
            pl.BlockSpec((d, tf), lambda i, f: (0, f + nf)),
            pl.BlockSpec((tf, d), lambda i, f: (f, 0)),
            pl.BlockSpec((1, d), lambda i, f: (0, 0)),
        ],
        out_specs=pl.BlockSpec((tm, d), lambda i, f: (i, 0)),
        out_shape=jax.ShapeDtypeStruct((t, d), F32),
        scratch_shapes=[pltpu.VMEM((tm, d), BF16)],
        compiler_params=_params("parallel", "arbitrary"),
        name="ffn_final" if final else "ffn",
    )(x, g.reshape(1, d), w_in, w_in, w_out, final_g.reshape(1, d))


def _nmm_kernel(x_ref, g_ref, w_ref, *rest):
    *o_refs, xn_ref = rest
    j = pl.program_id(1)

    @pl.when(j == 0)
    def _():
        xn_ref[...] = _rmsnorm(x_ref[...], g_ref[...]).astype(BF16)

    y = _dot(xn_ref[...], w_ref[...])
    if len(o_refs) == 1 and len(o_refs[0].shape) == 2:
        o_refs[0][...] = y
        return
    for s, o_ref in enumerate(o_refs):
        @pl.when(j == s)
        def _():
            for n in range(o_ref.shape[0]):
                o_ref[n] = y[:, n * LANES:(n + 1) * LANES]


def _normed_matmul(x, g, w, *, tn, name):
    t, d = x.shape
    n = w.shape[1]
    assert n % tn == 0 and tn % LANES == 0
    tm = _pick(t, 1024)
    return pl.pallas_call(
        _nmm_kernel,
        grid=(t // tm, n // tn),
        in_specs=[
            pl.BlockSpec((tm, d), lambda i, j: (i, 0)),
            pl.BlockSpec((1, d), lambda i, j: (0, 0)),
            pl.BlockSpec((d, tn), lambda i, j: (0, j)),
        ],
        out_specs=pl.BlockSpec((tm, tn), lambda i, j: (i, j)),
        out_shape=jax.ShapeDtypeStruct((t, n), F32),
        scratch_shapes=[pltpu.VMEM((tm, d), BF16)],
        compiler_params=_params("parallel", "arbitrary"),
        name=name,
    )(x, g.reshape(1, d), w)


def _normed_matmul_heads(x, g, w, *, sections, batch, seq, name):
    t, d = x.shape
    tn = w.shape[1] // sections
    heads = tn // LANES
    assert t == batch * seq and tn % LANES == 0
    tm = _pick(seq, 1024)
    nt = seq // tm
    out_spec = pl.BlockSpec((None, heads, tm, LANES), lambda i, j: (i // nt, 0, i % nt, 0))
    return pl.pallas_call(
        _nmm_kernel,
        grid=(t // tm, sections),
        in_specs=[
            pl.BlockSpec((tm, d), lambda i, j: (i, 0)),
            pl.BlockSpec((1, d), lambda i, j: (0, 0)),
            pl.BlockSpec((d, tn), lambda i, j: (0, j)),
        ],
        out_specs=[out_spec] * sections,
        out_shape=[jax.ShapeDtypeStruct((batch, heads, seq, LANES), F32)] * sections,
        scratch_shapes=[pltpu.VMEM((tm, d), BF16)],
        compiler_params=_params("parallel", "arbitrary"),
        name=name,
    )(x, g.reshape(1, d), w)


def _sb_suffix(z, valid, u):
    lr = -_softplus(z)
    if valid is not None:
        lr = jnp.where(valid, lr, 0.0)
    hi = lr.astype(BF16)
    lo = (lr - hi.astype(F32)).astype(BF16)
    return _dot(hi, u) + _dot(lo, u)


def _sb_exp(e, valid):
    w = jnp.exp(e)
    if valid is not None:
        w = jnp.where(valid, w, 0.0)
    return w.astype(BF16)


def _suffix_matrix(w_keys, totals=False):
    n = w_keys + (LANES if totals else 0)
    j = lax.broadcasted_iota(jnp.int32, (w_keys, n), 0)
    s = lax.broadcasted_iota(jnp.int32, (w_keys, n), 1)
    return ((j >= s) | (s >= w_keys)).astype(BF16)


def _sb_prompt_kernel(bias_ref, q_ref, k_ref, v_ref, u_ref, o_ref, kbf, vbf, acc, car, *, scale, blk, unroll):
    h = pl.program_id(1)
    i = pl.program_id(2)

    @pl.when(i == 0)
    def _():
        kbf[...] = k_ref[...].astype(BF16)
        vbf[...] = v_ref[...].astype(BF16)

    q = (q_ref[...] * scale).astype(BF16)
    bias = bias_ref[h]
    u = u_ref[...]
    rows = lax.broadcasted_iota(jnp.int32, (blk, blk), 0)
    cols = lax.broadcasted_iota(jnp.int32, (blk, blk), 1)

    def blocks(js, valid, carry):
        offs = [pl.multiple_of(j * blk, blk) for j in js]
        zs = [lax.dot_general(q, kbf[pl.ds(off, blk), :], _NT, preferred_element_type=F32) + bias
              for off in offs]
        sufs = [_sb_suffix(z, valid, u) for z in zs]
        pv = None
        for off, z, suf in zip(offs, zs, sufs):
            r = suf + jnp.concatenate([carry] * (blk // LANES), axis=1)
            carry = jnp.broadcast_to(r[:, 0:1], carry.shape)
            term = _dot(_sb_exp(z + r, valid), vbf[pl.ds(off, blk), :])
            pv = term if pv is None else pv + term
        return pv, carry

    pv, c = blocks([i], cols < rows, jnp.zeros((blk, LANES), F32))
    acc[...] = pv
    car[...] = c

    def group(n, _):
        j = i - 1 - unroll * n
        pv, c = blocks([j - m for m in range(unroll)], None, car[...])
        acc[...] += pv
        car[...] = c
        return 0

    lax.fori_loop(0, i // unroll, group, 0)

    def single(n, _):
        pv, c = blocks([i % unroll - 1 - n], None, car[...])
        acc[...] += pv
        car[...] = c
        return 0

    lax.fori_loop(0, i % unroll, single, 0)
    o_ref[...] = acc[...].astype(o_ref.dtype)


def _sb_prompt(q, k, v, bias):
    batch, heads, seq, _ = q.shape
    blk = _pick(seq, 256)
    assert blk % LANES == 0 and seq % blk == 0
    nq = seq // blk
    whole = pl.BlockSpec((None, None, seq, LANES), lambda b, h, i: (b, h, 0, 0))
    return pl.pallas_call(
        functools.partial(_sb_prompt_kernel, scale=1.0 / math.sqrt(LANES), blk=blk, unroll=4),
        grid=(batch, heads, nq),
        in_specs=[
            pl.BlockSpec(memory_space=pltpu.SMEM),
            pl.BlockSpec((None, None, blk, LANES), lambda b, h, i: (b, h, i, 0)),
            whole, whole,
            pl.BlockSpec((blk, blk), lambda b, h, i: (0, 0)),
        ],
        out_specs=pl.BlockSpec((blk, LANES), lambda b, h, i: (b * nq + i, h)),
        out_shape=jax.ShapeDtypeStruct((batch * seq, heads * LANES), BF16),
        scratch_shapes=[
            pltpu.VMEM((seq, LANES), BF16),
            pltpu.VMEM((seq, LANES), BF16),
            pltpu.VMEM((blk, LANES), F32),
            pltpu.VMEM((blk, LANES), F32),
        ],
        compiler_params=_params("parallel", "parallel", "arbitrary"),
        name="sb_prompt",
    )(bias, q, k, v, _suffix_matrix(blk))


def _sb_sample_kernel(pt_ref, q_ref, bias_ref, tpos_ref, u_ref, kn_ref, vn_ref, *rest, pages):
    k_refs = rest[:pages]
    v_refs = rest[pages:2 * pages]
    o_ref, acc, car = rest[2 * pages:]
    s = pl.program_id(1)
    heads, rph, _ = q_ref.shape
    bias = bias_ref[...]
    u = u_ref[...]

    def logits(k_ref):
        return jnp.concatenate(
            [lax.dot_general(q_ref[h], k_ref[h].astype(BF16), _NT, preferred_element_type=F32)
             for h in range(heads)], axis=0) + bias

    def weighted(z, rc, carry, v_ref, valid):
        w = _sb_exp(z + rc[:, :LANES] + carry, valid)
        return jnp.concatenate(
            [_dot(w[h * rph:(h + 1) * rph], v_ref[h].astype(BF16)) for h in range(heads)], axis=0)

    @pl.when(s == 0)
    def _():
        valid = lax.broadcasted_iota(jnp.int32, tpos_ref.shape, 1) < tpos_ref[...]
        z = logits(kn_ref)
        rc = _sb_suffix(z, valid, u)
        acc[...] = weighted(z, rc, jnp.zeros_like(z), vn_ref, valid)
        car[...] = rc[:, LANES:]

    zs = [logits(k_refs[p]) for p in range(pages)]
    rcs = [_sb_suffix(z, None, u) for z in zs]
    carry = car[...]
    carries = []
    for rc in rcs:
        carries.append(carry)
        carry = carry + rc[:, LANES:]
    car[...] = carry
    pvs = [weighted(z, rc, c, v_refs[p], None) for p, (z, rc, c) in enumerate(zip(zs, rcs, carries))]
    acc[...] += functools.reduce(lambda a, b: a + b, pvs)

    @pl.when(s == pl.num_programs(1) - 1)
    def _():
        o_ref[...] = acc[...]


def _sb_sample(q, k_new, v_new, cache_k, cache_v, page_table, bias):
    h, b, t, _ = q.shape
    n_phys, page = cache_k.shape[:2]
    n_pages = page_table.shape[1]
    assert page == LANES and t <= page
    rph = _round_up(t, SUBLANES)
    rows = h * rph
    pages = _divisor(n_pages, 8)
    scale = 1.0 / math.sqrt(LANES)
    per_head = lambda a, n: jnp.pad(a.transpose(1, 0, 2, 3), ((0, 0), (0, 0), (0, n - t), (0, 0)))
    q_rows = per_head(q * scale, rph).astype(BF16)
    bias_rows = jnp.broadcast_to(jnp.repeat(bias.astype(F32), rph)[:, None], (rows, LANES))
    tpos = jnp.broadcast_to(jnp.tile(jnp.arange(rph, dtype=jnp.int32), h)[:, None], (rows, LANES))
    tpos = jnp.minimum(tpos, t)
    ck = cache_k.transpose(0, 2, 1, 3)
    cv = cache_v.transpose(0, 2, 1, 3)

    def page_spec(p):
        return pl.BlockSpec((None, h, page, LANES),
                            lambda i, s, pt: (pt[i, n_pages - 1 - (s * pages + p)], 0, 0, 0))

    const = lambda shape: pl.BlockSpec(shape, lambda i, s, pt: (0,) * len(shape))
    per_b = lambda *shape: pl.BlockSpec((None,) + shape, lambda i, s, pt: (i,) + (0,) * len(shape))
    o_rows = pl.pallas_call(
        functools.partial(_sb_sample_kernel, pages=pages),
        grid_spec=pltpu.PrefetchScalarGridSpec(
            num_scalar_prefetch=1,
            grid=(b, n_pages // pages),
            in_specs=[per_b(h, rph, LANES), const((rows, LANES)), const((rows, LANES)),
                      const((page, page + LANES)), per_b(h, page, LANES), per_b(h, page, LANES)]
                     + [page_spec(p) for p in range(pages)] * 2,
            out_specs=per_b(rows, LANES),
            scratch_shapes=[pltpu.VMEM((rows, LANES), F32), pltpu.VMEM((rows, LANES), F32)],
        ),
        out_shape=jax.ShapeDtypeStruct((b, rows, LANES), F32),
        compiler_params=_params("parallel", "arbitrary"),
        name="sb_sample",
    )(page_table, q_rows, bias_rows, tpos, _suffix_matrix(page, totals=True),
      per_head(k_new, page), per_head(v_new, page),
      *([ck] * pages), *([cv] * pages))
    return o_rows.reshape(b, h, rph, LANES)[:, :, :t].transpose(0, 2, 1, 3)


def _dot_hi(a, b):
    return jnp.dot(a, b, precision=HIGHEST, preferred_element_type=F32)


def _split(a):
    hi = a.astype(BF16)
    return hi, (a - hi.astype(F32)).astype(BF16)


def _dot3(a, b):
    return _dot(a[0], b[0]) + _dot(a[0], b[1]) + _dot(a[1], b[0])


def _gdn_prep_kernel(raw_ref, prev_ref, buf_ref, a_ref, b_ref, cw_ref, alog_ref, dtb_ref,
                     u_ref, w_ref, qd_ref, kd_ref, qk_ref, egl_ref, xfull,
                     *, heads, conv_k, chunk, valid_last):
    c = pl.program_id(1)
    rows = raw_ref.shape[0]
    n_chunks = rows // chunk
    gw = heads * LANES

    @pl.when(c == 0)
    def _():
        xfull[0:SUBLANES, :] = buf_ref[...]

    @pl.when(c > 0)
    def _():
        xfull[0:SUBLANES, :] = prev_ref[...]

    xfull[SUBLANES:SUBLANES + rows, :] = raw_ref[...]
    y = None
    for j in range(conv_k):
        start = SUBLANES - (conv_k - 1) + j
        term = xfull[start:start + rows, :] * cw_ref[j:j + 1, :]
        y = term if y is None else y + term
    y = _silu(y)

    g = -jnp.exp(alog_ref[...]) * _softplus(a_ref[...] + dtb_ref[...])
    beta = jax.nn.sigmoid(b_ref[...])
    live = None
    if valid_last < chunk:
        last = c == pl.num_programs(1) - 1
        row = lax.broadcasted_iota(jnp.int32, (rows, 1), 0)
        live_rows = row < jnp.where(last, rows - chunk + valid_last, rows)
        g = jnp.where(live_rows, g, 0.0)
        beta = jnp.where(live_rows, beta, 0.0)
        live = lax.broadcasted_iota(jnp.int32, (chunk, 1), 0) < jnp.where(last, valid_last, chunk)

    shift = int(math.log2(chunk))
    ri = lax.broadcasted_iota(jnp.int32, (rows, rows), 0)
    ci = lax.broadcasted_iota(jnp.int32, (rows, rows), 1)
    same = (ri >> shift) == (ci >> shift)
    gc = _dot_hi(jnp.where(same & (ri >= ci), 1.0, 0.0), g)
    sel = (lax.broadcasted_iota(jnp.int32, (SUBLANES, LANES), 0)
           == lax.broadcasted_iota(jnp.int32, (SUBLANES, LANES), 1)).astype(F32)
    gc_rows = lax.dot_general(sel, gc, _NT, precision=HIGHEST, preferred_element_type=F32)

    ri = lax.broadcasted_iota(jnp.int32, (chunk, chunk), 0)
    ci = lax.broadcasted_iota(jnp.int32, (chunk, chunk), 1)
    incl = ri >= ci
    strict = ri > ci

    for n in range(n_chunks):
        egl_ref[n] = jnp.broadcast_to(jnp.exp(gc[(n + 1) * chunk - 1:(n + 1) * chunk, :]), (SUBLANES, LANES))

    chains = [(n, h) for n in range(n_chunks) for h in range(heads)]
    lows, rhss = [], []
    for n, h in chains:
        rs = slice(n * chunk, (n + 1) * chunk)
        sl = slice(h * LANES, (h + 1) * LANES)
        qh = y[rs, h * LANES:(h + 1) * LANES]
        kh = y[rs, gw + h * LANES:gw + (h + 1) * LANES]
        vh = y[rs, 2 * gw + h * LANES:2 * gw + (h + 1) * LANES]
        qh = qh * lax.rsqrt(jnp.sum(qh * qh, axis=-1, keepdims=True) + NORM_EPS) * (LANES ** -0.5)
        kh = kh * lax.rsqrt(jnp.sum(kh * kh, axis=-1, keepdims=True) + NORM_EPS)
        if live is not None and n == n_chunks - 1:
            qh = jnp.where(live, qh, 0.0)
            kh = jnp.where(live, kh, 0.0)
            vh = jnp.where(live, vh, 0.0)
        g_col = gc[rs, h:h + 1]
        g_row = gc_rows[h:h + 1, rs]
        g_last = gc[(n + 1) * chunk - 1:(n + 1) * chunk, h:h + 1]
        decay = jnp.exp(jnp.where(incl, g_col - g_row, 0.0))
        b_col = beta[rs, h:h + 1]
        eg = jnp.exp(g_col)
        kb = kh * b_col
        k_bf = kh.astype(BF16)
        low = lax.dot_general(kb.astype(BF16), k_bf, _NT, preferred_element_type=F32)
        lows.append(low * jnp.where(strict, decay, 0.0))
        rhss.append(_split(jnp.concatenate([vh * b_col, kb * eg], axis=1)))
        qk = lax.dot_general(qh.astype(BF16), k_bf, _NT, preferred_element_type=F32)
        qk_ref[rs, h * chunk:(h + 1) * chunk] = qk * jnp.where(incl, decay, 0.0)
        qd_ref[rs, sl] = (qh * eg).astype(BF16)
        kd_ref[rs, sl] = (kh * jnp.exp(g_last - g_col)).astype(BF16)

    eye_c = jnp.where(ri == ci, 1.0, 0.0)
    pws = [_split(low) for low in lows]
    invs = [eye_c - low for low in lows]
    for _ in range(shift - 1):
        pws = [_split(_dot3(pw, pw)) for pw in pws]
        invs = [inv + _dot3(_split(inv), pw) for inv, pw in zip(invs, pws)]
    for (n, h), inv, rhs in zip(chains, invs, rhss):
        rs = slice(n * chunk, (n + 1) * chunk)
        sl = slice(h * LANES, (h + 1) * LANES)
        x = _dot3(_split(inv), rhs)
        u_ref[rs, sl] = x[:, :LANES]
        w_ref[rs, sl] = x[:, LANES:].astype(BF16)


def _gdn_scan_kernel(u_ref, w_ref, qd_ref, kd_ref, qk_ref, egl_ref, z_ref, s0_ref, on_ref,
                     o_ref, s_ref, *, heads, chunk):
    bt, rows, _ = u_ref.shape

    @pl.when(pl.program_id(1) == 0)
    def _():
        s_ref[...] = s0_ref[...]

    chains = [(bi, h, slice(h * LANES, (h + 1) * LANES)) for bi in range(bt) for h in range(heads)]
    for n in range(rows // chunk):
        rs = slice(n * chunk, (n + 1) * chunk)
        s_old = [s_ref[bi, h] for bi, h, _ in chains]
        s_bf = [s.astype(BF16) for s in s_old]
        ws = [_dot(w_ref[bi, rs, sl], s) for (bi, _, sl), s in zip(chains, s_bf)]
        qs = [_dot(qd_ref[bi, rs, sl], s) for (bi, _, sl), s in zip(chains, s_bf)]
        v_new = [(u_ref[bi, rs, sl] - x).astype(BF16) for (bi, _, sl), x in zip(chains, ws)]
        os = [x + _dot(qk_ref[bi, rs, h * chunk:(h + 1) * chunk].astype(BF16), v)
              for (bi, h, _), x, v in zip(chains, qs, v_new)]
        upd = [lax.dot_general(kd_ref[bi, rs, sl], v, _TN, preferred_element_type=F32)
               for (bi, _, sl), v in zip(chains, v_new)]
        for (bi, h, sl), s, d, o_h in zip(chains, s_old, upd, os):
            s_ref[bi, h] = s * egl_ref[bi, n, 0:1, h:h + 1] + d
            o_h = _rmsnorm(o_h, on_ref[...]) * _silu(z_ref[bi, rs, sl])
            o_ref[bi, rs, sl] = o_h.astype(o_ref.dtype)


def _gdn(raw, a, b, z, conv_buf, s0, conv_w, a_log, dt_bias, out_norm, *, batch, seq, valid_last):
    heads = s0.shape[1]
    gw = heads * LANES
    conv_k = conv_w.shape[0]
    chunk = GDN_CHUNK
    assert seq % chunk == 0 and heads <= SUBLANES and conv_k - 1 <= SUBLANES
    nc = seq // chunk
    (raw_a, raw_c), (a_a, a_c), (b_a, b_c), (z_a, z_c) = raw, a, b, z
    lane_vec = lambda v: jnp.pad(v.astype(F32), (0, LANES - heads)).reshape(1, LANES)
    buf = jnp.pad(conv_buf, ((0, 0), (SUBLANES - (conv_k - 1), 0), (0, 0)))

    cps = _divisor(nc, 4)
    rows = cps * chunk
    ns = nc // cps
    tok = lambda width, col: pl.BlockSpec((rows, width), lambda i, c: (i * ns + c, col))
    const = lambda shape: pl.BlockSpec(shape, lambda i, c: (0,) * len(shape))
    tokens = lambda width, dtype: jax.ShapeDtypeStruct((batch * seq, width), dtype)
    u, w, qd, kd, qk, egl = pl.pallas_call(
        functools.partial(_gdn_prep_kernel, heads=heads, conv_k=conv_k, chunk=chunk, valid_last=valid_last),
        grid=(batch, ns),
        in_specs=[
            tok(3 * gw, raw_c),
            pl.BlockSpec((SUBLANES, 3 * gw),
                         lambda i, c: (jnp.maximum((i * ns + c) * (rows // SUBLANES) - 1, 0), raw_c)),
            pl.BlockSpec((None, SUBLANES, 3 * gw), lambda i, c: (i, 0, 0)),
            tok(LANES, a_c), tok(LANES, b_c),
            const((conv_k, 3 * gw)), const((1, LANES)), const((1, LANES)),
        ],
        out_specs=[tok(gw, 0), tok(gw, 0), tok(gw, 0), tok(gw, 0), tok(heads * chunk, 0),
                   pl.BlockSpec((cps, SUBLANES, LANES), lambda i, c: (i * ns + c, 0, 0))],
        out_shape=[tokens(gw, F32), tokens(gw, BF16), tokens(gw, BF16), tokens(gw, BF16),
                   tokens(heads * chunk, F32),
                   jax.ShapeDtypeStruct((batch * nc, SUBLANES, LANES), F32)],
        scratch_shapes=[pltpu.VMEM((SUBLANES + rows, 3 * gw), F32)],
        compiler_params=_params("parallel", "arbitrary"),
        name="gdn_prep",
    )(raw_a, raw_a, buf, a_a, b_a, conv_w, lane_vec(a_log), lane_vec(dt_bias))

    bt = 2 if batch % 2 == 0 else 1
    nz = z_a.shape[1]
    seq3 = lambda arr: arr.reshape(batch, seq, arr.shape[-1])
    tok3 = lambda width, col=0: pl.BlockSpec((bt, rows, width), lambda i, c: (i, c, col))
    state = pl.BlockSpec((bt, heads, LANES, LANES), lambda i, c: (i, 0, 0, 0))
    o, s_new = pl.pallas_call(
        functools.partial(_gdn_scan_kernel, heads=heads, chunk=chunk),
        grid=(batch // bt, ns),
        in_specs=[tok3(gw), tok3(gw), tok3(gw), tok3(gw), tok3(heads * chunk),
                  pl.BlockSpec((bt, cps, SUBLANES, LANES), lambda i, c: (i, c, 0, 0)),
                  tok3(gw, z_c), state, const((1, LANES))],
        out_specs=[tok3(gw), state],
        out_shape=[jax.ShapeDtypeStruct((batch, seq, gw), BF16), jax.ShapeDtypeStruct(s0.shape, F32)],
        compiler_params=_params("parallel", "arbitrary"),
        name="gdn_scan",
    )(seq3(u), seq3(w), seq3(qd), seq3(kd), seq3(qk), egl.reshape(batch, nc, SUBLANES, LANES),
      z_a.reshape(batch, seq, nz), s0.astype(F32), out_norm.astype(F32).reshape(1, LANES))
    return o.reshape(batch * seq, gw), s_new


def _mem_attn_kernel(q_ref, k_ref, v_ref, o_ref, *, scale):
    for h in range(k_ref.shape[0]):
        sl = slice(h * LANES, (h + 1) * LANES)
        q = (q_ref[:, sl] * scale).astype(BF16)
        z = lax.dot_general(q, k_ref[h].astype(BF16), _NT, preferred_element_type=F32)
        e = jnp.exp(z - jnp.max(z, axis=-1, keepdims=True))
        p = e / jnp.sum(e, axis=-1, keepdims=True)
        o_ref[:, sl] = _dot(p.astype(BF16), v_ref[h].astype(BF16)).astype(o_ref.dtype)


def _mem_attn(q, mem_k, mem_v, *, batch, seq):
    q_a, q_c = q
    _, heads, n_mem, _ = mem_k.shape
    xw = heads * LANES
    tq = _pick(seq, 512)
    nq = seq // tq
    mem_spec = pl.BlockSpec((None, heads, n_mem, LANES), lambda i, j: (i, 0, 0, 0))
    return pl.pallas_call(
        functools.partial(_mem_attn_kernel, scale=1.0 / math.sqrt(LANES)),
        grid=(batch, nq),
        in_specs=[pl.BlockSpec((tq, xw), lambda i, j: (i * nq + j, q_c)), mem_spec, mem_spec],
        out_specs=pl.BlockSpec((tq, xw), lambda i, j: (i * nq + j, 0)),
        out_shape=jax.ShapeDtypeStruct((batch * seq, xw), BF16),
        compiler_params=_params("parallel", "parallel"),
        name="mem_attn",
    )(q_a, mem_k, mem_v)


def _merge_kernel(x_ref, osb_ref, ogdn_ref, ox_ref, gsb_ref, ggdn_ref, gx_ref,
                  wsb_ref, wgdn_ref, wx_ref, wout_ref, o_ref):
    merged = (jax.nn.sigmoid(gsb_ref[...]) * _dot(osb_ref[...], wsb_ref[...])
              + jax.nn.sigmoid(ggdn_ref[...]) * _dot(ogdn_ref[...], wgdn_ref[...])
              + jax.nn.sigmoid(gx_ref[...]) * _dot(ox_ref[...], wx_ref[...]))
    o_ref[...] = x_ref[...] + _dot(merged.astype(BF16), wout_ref[...])


def _merge(x, o_sb, o_gdn, o_x, proj, gate_col, w_up_sb, w_up_gdn, w_up_x, w_out):
    t, d = x.shape
    tm = _pick(t, 256)
    tok = lambda a: pl.BlockSpec((tm, a.shape[1]), lambda i: (i, 0))
    gate = lambda n: pl.BlockSpec((tm, d), lambda i: (i, gate_col + n))
    weight = lambda w: pl.BlockSpec(w.shape, lambda i: (0, 0), pipeline_mode=pl.Buffered(1))
    return pl.pallas_call(
        _merge_kernel,
        grid=(t // tm,),
        in_specs=[tok(x), tok(o_sb), tok(o_gdn), tok(o_x), gate(0), gate(1), gate(2),
                  weight(w_up_sb), weight(w_up_gdn), weight(w_up_x), weight(w_out)],
        out_specs=tok(x),
        out_shape=jax.ShapeDtypeStruct((t, d), F32),
        compiler_params=_params("parallel"),
        name="merge",
    )(x, o_sb, o_gdn, o_x, proj, proj, proj, w_up_sb, w_up_gdn, w_up_x, w_out)


def _split_w_in(w_in, sbw, gw, xw, d, g_heads):
    w_sb = w_in[:, :3 * sbw]
    o = 3 * sbw
    w_gqkv = w_in[:, o:o + 3 * gw]
    o += 3 * gw
    w_a = w_in[:, o:o + g_heads]
    w_b = w_in[:, o + g_heads:o + 2 * g_heads]
    o += 2 * g_heads
    w_z = w_in[:, o:o + gw]
    w_xq = w_in[:, o + gw:o + gw + xw]
    gates = w_in[:, o + gw + xw:]
    assert gates.shape[1] == 3 * d
    pad = lambda w: jnp.pad(w, ((0, 0), (0, LANES - g_heads)))
    rest = jnp.concatenate([gates, w_z, w_gqkv, w_xq, pad(w_a), pad(w_b)], axis=1)
    return w_sb.astype(BF16), rest.astype(BF16)


def kernel(x_prompt, x_sample, cache_sb_k, cache_sb_v, cache_mem_k, cache_mem_v, state_gdn_S, state_gdn_conv, page_table, mem_prompt, ffn1_norm, ffn1_w_in, ffn1_w_out, mix_norm, w_in, sb_logit_bias, gdn_conv_w, gdn_a_log, gdn_dt_bias, gdn_out_norm, mem_norm, w_mem_kv, w_up_sb, w_up_gdn, w_up_x, w_out, ffn2_norm, ffn2_w_in, ffn2_w_out, final_norm):
    batch, seq, d = x_prompt.shape
    dec_batch, dec_seq, _ = x_sample.shape
    depth = w_in.shape[0]
    sb_heads = cache_sb_k.shape[3]
    x_heads = cache_mem_k.shape[3]
    g_heads = state_gdn_S.shape[2]
    n_mem = mem_prompt.shape[1]
    conv_k = gdn_conv_w.shape[1]
    sbw, gw, xw = sb_heads * LANES, g_heads * LANES, x_heads * LANES

    off_gate, off_z = 0, 3 * d
    off_gdn = off_z + gw
    off_xq = off_gdn + 3 * gw
    off_a, off_b = off_xq + xw, off_xq + xw + LANES
    n_proj = off_b + LANES
    assert off_z % gw == 0 and off_gdn % (3 * gw) == 0 and off_xq % xw == 0
    tn_proj = 3 * 256
    assert n_proj % tn_proj == 0

    xp = x_prompt.reshape(batch * seq, d)
    xs = x_sample.reshape(dec_batch * dec_seq, d)
    outs = [[] for _ in range(10)]
    for l in range(depth):
        bf = lambda w: w[l].astype(BF16)
        w1_in, w1_out, w2_in, w2_out = bf(ffn1_w_in), bf(ffn1_w_out), bf(ffn2_w_in), bf(ffn2_w_out)
        w_sb, w_rest = _split_w_in(w_in[l], sbw, gw, xw, d, g_heads)
        wup_sb, wup_gdn, wup_x, wo = bf(w_up_sb), bf(w_up_gdn), bf(w_up_x), bf(w_out)
        last = l == depth - 1
        mixer_w = (gdn_conv_w[l], gdn_a_log[l], gdn_dt_bias[l], gdn_out_norm[l])

        def mix_tail(x, proj, o_sb, o_gdn, o_x):
            x = _merge(x, o_sb, o_gdn, o_x, proj, off_gate // d, wup_sb, wup_gdn, wup_x, wo)
            return _ffn(x, ffn2_norm[l], w2_in, w2_out, final_norm, final=last)

        mk_p, mv_p = _normed_matmul_heads(mem_prompt.reshape(batch * n_mem, d), mem_norm[l], bf(w_mem_kv),
                                          sections=2, batch=batch, seq=n_mem, name="mem_kv")
        xp = _ffn(xp, ffn1_norm[l], w1_in, w1_out, final_norm, final=False)
        q_p, k_p, v_p = _normed_matmul_heads(xp, mix_norm[l], w_sb, sections=3, batch=batch, seq=seq,
                                             name="sb_proj")
        proj = _normed_matmul(xp, mix_norm[l], w_rest, tn=tn_proj, name="mixer_proj")
        o_sb = _sb_prompt(q_p, k_p, v_p, sb_logit_bias[l].astype(F32))
        o_gdn, s_p = _gdn((proj, off_gdn // (3 * gw)), (proj, off_a // LANES), (proj, off_b // LANES),
                          (proj, off_z // gw), jnp.zeros((batch, conv_k - 1, 3 * gw), F32),
                          jnp.zeros((batch, g_heads, LANES, LANES), F32), *mixer_w,
                          batch=batch, seq=seq, valid_last=GDN_CHUNK)
        o_x = _mem_attn((proj, off_xq // xw), mk_p, mv_p, batch=batch, seq=seq)
        xp = mix_tail(xp, proj, o_sb, o_gdn, o_x)
        seq_major = lambda a: a.transpose(0, 2, 1, 3)
        outs[0].append(seq_major(k_p))
        outs[1].append(seq_major(v_p))
        outs[2].append(seq_major(mk_p))
        outs[3].append(seq_major(mv_p))
        outs[4].append(s_p)
        assert seq >= conv_k - 1
        outs[5].append(proj.reshape(batch, seq, n_proj)[:, seq - (conv_k - 1):, off_gdn:off_gdn + 3 * gw])

        xs = _ffn(xs, ffn1_norm[l], w1_in, w1_out, final_norm, final=False)
        qkv = _normed_matmul_heads(xs, mix_norm[l], w_sb, sections=3, batch=1, seq=dec_batch * dec_seq,
                                   name="sb_proj_sample")
        proj = _normed_matmul(xs, mix_norm[l], w_rest, tn=tn_proj, name="mixer_proj_sample")
        q_s, k_s, v_s = (a.reshape(sb_heads, dec_batch, dec_seq, LANES) for a in qkv)
        o_sb = _sb_sample(q_s, k_s, v_s, cache_sb_k[l], cache_sb_v[l], page_table, sb_logit_bias[l])
        o_sb = o_sb.reshape(dec_batch * dec_seq, sbw).astype(BF16)
        proj3 = proj.reshape(dec_batch, dec_seq, n_proj)
        seq_g = _round_up(dec_seq, GDN_CHUNK)
        seq_x = _round_up(dec_seq, SUBLANES)
        padded = lambda lo, hi, n: jnp.pad(proj3[:, :, lo:hi], ((0, 0), (0, n - dec_seq), (0, 0))
                                           ).reshape(dec_batch * n, hi - lo)
        raw_s = proj3[:, :, off_gdn:off_gdn + 3 * gw]
        o_gdn, s_s = _gdn((padded(off_gdn, off_gdn + 3 * gw, seq_g), 0), (padded(off_a, off_a + LANES, seq_g), 0),
                          (padded(off_b, off_b + LANES, seq_g), 0), (padded(off_z, off_z + gw, seq_g), 0),
                          state_gdn_conv[l], state_gdn_S[l], *mixer_w,
                          batch=dec_batch, seq=seq_g, valid_last=dec_seq - (seq_g - GDN_CHUNK))
        o_gdn = o_gdn.reshape(dec_batch, seq_g, gw)[:, :dec_seq].reshape(dec_batch * dec_seq, gw)
        o_x = _mem_attn((padded(off_xq, off_xq + xw, seq_x), 0), cache_mem_k[l].transpose(0, 2, 1, 3),
                        cache_mem_v[l].transpose(0, 2, 1, 3), batch=dec_batch, seq=seq_x)
        o_x = o_x.reshape(dec_batch, seq_x, xw)[:, :dec_seq].reshape(dec_batch * dec_seq, xw)
        xs = mix_tail(xs, proj, o_sb, o_gdn, o_x)
        outs[6].append(k_s.transpose(1, 2, 0, 3))
        outs[7].append(v_s.transpose(1, 2, 0, 3))
        outs[8].append(s_s)
        conv_all = jnp.concatenate([state_gdn_conv[l].astype(F32), raw_s], axis=1)
        outs[9].append(conv_all[:, dec_seq:])

    stack = [jnp.stack(o) for o in outs]
    return (xp.reshape(batch, seq, d), xs.reshape(dec_batch, dec_seq, d), *stack)
```

```python
import functools
import math

import jax
import jax.numpy as jnp
from jax import lax
from jax.experimental import pallas as pl
from jax.experimental.pallas import tpu as pltpu

F32 = jnp.float32
BF16 = jnp.bfloat16
NORM_EPS = 1e-6
LANES = 128
SUBLANES = 8
GDN_CHUNK = 64
VMEM_LIMIT = 56 * 1024 * 1024
HIGHEST = lax.Precision.HIGHEST

_NT = (((1,), (1,)), ((), ()))
_TN = (((0,), (0,)), ((), ()))


def _params(*sem):
    return pltpu.CompilerParams(dimension_semantics=sem, vmem_limit_bytes=VMEM_LIMIT)


def _dot(a, b):
    return jnp.dot(a, b, preferred_element_type=F32)


def _rmsnorm(x, g):
    return x * lax.rsqrt(jnp.mean(x * x, axis=-1, keepdims=True) + NORM_EPS) * g


def _softplus(x):
    return jnp.maximum(x, 0.0) + jnp.log(1.0 + jnp.exp(-jnp.abs(x)))


def _silu(x):
    return x * jax.nn.sigmoid(x)


def _pick(n, pref):
    if n <= pref:
        return n
    t = pref
    while n % t:
        t //= 2
    assert t >= SUBLANES, (n, pref)
    return t


def _divisor(n, pref):
    return max(k for k in range(1, min(n, pref) + 1) if n % k == 0)


def _round_up(n, m):
    return -(-n // m) * m


def _ffn_kernel(x_ref, g_ref, wg_ref, wu_ref, wo_ref, fg_ref, o_ref, xn_ref, *, final):
    f = pl.program_id(1)

    @pl.when(f == 0)
    def _():
        xn_ref[...] = _rmsnorm(x_ref[...], g_ref[...]).astype(BF16)
        o_ref[...] = jnp.zeros_like(o_ref)

    xn = xn_ref[...]
    h = (_silu(_dot(xn, wg_ref[...])) * _dot(xn, wu_ref[...])).astype(BF16)
    o_ref[...] += _dot(h, wo_ref[...])

    @pl.when(f == pl.num_programs(1) - 1)
    def _():
        y = x_ref[...] + 0.5 * o_ref[...]
        if final:
            y = _rmsnorm(y, fg_ref[...])
        o_ref[...] = y


def _ffn(x, g, w_in, w_out, final_g, *, final):
    t, d = x.shape
    dff = w_out.shape[0]
    tm = _pick(t, 512)
    tf = _pick(dff, 512)
    nf = dff // tf
    return pl.pallas_call(
        functools.partial(_ffn_kernel, final=final),
        grid=(t // tm, nf),
        in_specs=[
            pl.BlockSpec((tm, d), lambda i, f: (i, 0)),
            pl.BlockSpec((1, d), lambda i, f: (0, 0)),
            pl.BlockSpec((d, tf), lambda i, f: (0, f)),
            pl.BlockSpec((d, tf), lambda i, f: (0, f + nf)),
            pl.BlockSpec((tf, d), lambda i, f: (f, 0)),
            pl.BlockSpec((1, d), lambda i, f: (0, 0)),
        ],
        out_specs=pl.BlockSpec((tm, d), lambda i, f: (i, 0)),
        out_shape=jax.ShapeDtypeStruct((t, d), F32),
        scratch_shapes=[pltpu.VMEM((tm, d), BF16)],
        compiler_params=_params("parallel", "arbitrary"),
        name="ffn_final" if final else "ffn",
    )(x, g.reshape(1, d), w_in, w_in, w_out, final_g.reshape(1, d))


def _nmm_kernel(x_ref, g_ref, w_ref, *rest):
    *o_refs, xn_ref = rest
    j = pl.program_id(1)

    @pl.when(j == 0)
    def _():
        xn_ref[...] = _rmsnorm(x_ref[...], g_ref[...]).astype(BF16)

    y = _dot(xn_ref[...], w_ref[...])
    if len(o_refs) == 1 and len(o_refs[0].shape) == 2:
        o_refs[0][...] = y
        return
    for s, o_ref in enumerate(o_refs):
        @pl.when(j == s)
        def _():
            for n in range(o_ref.shape[0]):
                o_ref[n] = y[:, n * LANES:(n + 1) * LANES]


def _normed_matmul(x, g, w, *, tn, name):
    t, d = x.shape
    n = w.shape[1]
    assert n % tn == 0 and tn % LANES == 0
    tm = _pick(t, 1024)
    return pl.pallas_call(
        _nmm_kernel,
        grid=(t // tm, n // tn),
        in_specs=[
            pl.BlockSpec((tm, d), lambda i, j: (i, 0)),
            pl.BlockSpec((1, d), lambda i, j: (0, 0)),
            pl.BlockSpec((d, tn), lambda i, j: (0, j)),
        ],
        out_specs=pl.BlockSpec((tm, tn), lambda i, j: (i, j)),
        out_shape=jax.ShapeDtypeStruct((t, n), F32),
        scratch_shapes=[pltpu.VMEM((tm, d), BF16)],
        compiler_params=_params("parallel", "arbitrary"),
        name=name,
    )(x, g.reshape(1, d), w)


def _normed_matmul_heads(x, g, w, *, sections, batch, seq, name):
    t, d = x.shape
    tn = w.shape[1] // sections
    heads = tn // LANES
    assert t == batch * seq and tn % LANES == 0
    tm = _pick(seq, 1024)
    nt = seq // tm
    out_spec = pl.BlockSpec((None, heads, tm, LANES), lambda i, j: (i // nt, 0, i % nt, 0))
    return pl.pallas_call(
        _nmm_kernel,
        grid=(t // tm, sections),
        in_specs=[
            pl.BlockSpec((tm, d), lambda i, j: (i, 0)),
            pl.BlockSpec((1, d), lambda i, j: (0, 0)),
            pl.BlockSpec((d, tn), lambda i, j: (0, j)),
        ],
        out_specs=[out_spec] * sections,
        out_shape=[jax.ShapeDtypeStruct((batch, heads, seq, LANES), F32)] * sections,
        scratch_shapes=[pltpu.VMEM((tm, d), BF16)],
        compiler_params=_params("parallel", "arbitrary"),
        name=name,
    )(x, g.reshape(1, d), w)


LOG2_E = math.log2(math.e)


def _sb_suffix(z, valid, u):
    sp = jnp.maximum(z, 0.0) + jnp.log(1.0 + jnp.exp2(jnp.abs(z) * -LOG2_E))
    if valid is not None:
        sp = jnp.where(valid, sp, 0.0)
    return _dot(sp.astype(BF16), u)


def _sb_exp(z, r, valid):
    w = jnp.exp(z - r)
    if valid is not None:
        w = jnp.where(valid, w, 0.0)
    return w.astype(BF16)


def _suffix_matrix(w_keys, totals=False):
    n = w_keys + (LANES if totals else 0)
    j = lax.broadcasted_iota(jnp.int32, (w_keys, n), 0)
    s = lax.broadcasted_iota(jnp.int32, (w_keys, n), 1)
    return ((j >= s) | (s >= w_keys)).astype(BF16)


def _sb_prompt_kernel(bias_ref, q_ref, k_ref, v_ref, u_ref, o_ref, kbf, vbf, acc, car, *, scale, blk, unroll):
    h = pl.program_id(1)
    i = pl.program_id(2)

    @pl.when(i == 0)
    def _():
        kbf[...] = k_ref[...].astype(BF16)
        vbf[...] = v_ref[...].astype(BF16)

    q = (q_ref[...] * scale).astype(BF16)
    bias = bias_ref[h]
    u = u_ref[...]
    rows = lax.broadcasted_iota(jnp.int32, (blk, blk), 0)
    cols = lax.broadcasted_iota(jnp.int32, (blk, blk), 1)

    def blocks(js, diagonal, carry):
        offs = [pl.multiple_of(j * blk, blk) for j in js]
        valids = [cols < rows if diagonal and n == 0 else None for n in range(len(js))]
        zs = [lax.dot_general(q, kbf[pl.ds(off, blk), :], _NT, preferred_element_type=F32) + bias
              for off in offs]
        sufs = [_sb_suffix(z, valid, u) for z, valid in zip(zs, valids)]
        pv = None
        for off, z, suf, valid in zip(offs, zs, sufs, valids):
            r = suf + jnp.concatenate([carry] * (blk // LANES), axis=1)
            carry = jnp.broadcast_to(r[:, 0:1], carry.shape)
            term = _dot(_sb_exp(z, r, valid), vbf[pl.ds(off, blk), :])
            pv = term if pv is None else pv + term
        return pv, carry

    for r in range(2):
        @pl.when(i % 2 == r)
        def _():
            pv, c = blocks([i - m for m in range(r + 1)], True, jnp.zeros((blk, LANES), F32))
            acc[...] = pv
            car[...] = c

    def group(j, size):
        pv, c = blocks([j - m for m in range(size)], False, car[...])
        acc[...] += pv
        car[...] = c

    j = i - 1 - i % 2
    size = 2
    while size < unroll:
        bit = (i // size) % 2

        @pl.when(bit == 1)
        def _():
            group(j, size)

        j = j - size * bit
        size *= 2

    def body(n, _):
        group(j - unroll * n, unroll)
        return 0

    lax.fori_loop(0, i // unroll, body, 0)
    o_ref[...] = acc[...].astype(o_ref.dtype)


def _sb_prompt(q, k, v, bias):
    batch, heads, seq, _ = q.shape
    blk = _pick(seq, 256)
    assert blk % LANES == 0 and seq % blk == 0
    nq = seq // blk
    whole = pl.BlockSpec((None, None, seq, LANES), lambda b, h, i: (b, h, 0, 0))
    return pl.pallas_call(
        functools.partial(_sb_prompt_kernel, scale=1.0 / math.sqrt(LANES), blk=blk, unroll=8),
        grid=(batch, heads, nq),
        in_specs=[
            pl.BlockSpec(memory_space=pltpu.SMEM),
            pl.BlockSpec((None, None, blk, LANES), lambda b, h, i: (b, h, i, 0)),
            whole, whole,
            pl.BlockSpec((blk, blk), lambda b, h, i: (0, 0)),
        ],
        out_specs=pl.BlockSpec((blk, LANES), lambda b, h, i: (b * nq + i, h)),
        out_shape=jax.ShapeDtypeStruct((batch * seq, heads * LANES), BF16),
        scratch_shapes=[
            pltpu.VMEM((seq, LANES), BF16),
            pltpu.VMEM((seq, LANES), BF16),
            pltpu.VMEM((blk, LANES), F32),
            pltpu.VMEM((blk, LANES), F32),
        ],
        compiler_params=_params("parallel", "parallel", "arbitrary"),
        name="sb_prompt",
    )(bias, q, k, v, _suffix_matrix(blk))


def _sb_sample_kernel(pt_ref, q_ref, bias_ref, tpos_ref, u_ref, kn_ref, vn_ref, *rest, pages):
    k_refs = rest[:pages]
    v_refs = rest[pages:2 * pages]
    o_ref, acc, car = rest[2 * pages:]
    s = pl.program_id(1)
    heads, rph, _ = q_ref.shape
    bias = bias_ref[...]
    u = u_ref[...]

    def logits(k_ref):
        return jnp.concatenate(
            [lax.dot_general(q_ref[h], k_ref[h].astype(BF16), _NT, preferred_element_type=F32)
             for h in range(heads)], axis=0) + bias

    def weighted(z, rc, carry, v_ref, valid):
        w = _sb_exp(z, rc[:, :LANES] + carry, valid)
        return jnp.concatenate(
            [_dot(w[h * rph:(h + 1) * rph], v_ref[h].astype(BF16)) for h in range(heads)], axis=0)

    @pl.when(s == 0)
    def _():
        valid = lax.broadcasted_iota(jnp.int32, tpos_ref.shape, 1) < tpos_ref[...]
        z = logits(kn_ref)
        rc = _sb_suffix(z, valid, u)
        acc[...] = weighted(z, rc, jnp.zeros_like(z), vn_ref, valid)
        car[...] = rc[:, LANES:]

    zs = [logits(k_refs[p]) for p in range(pages)]
    rcs = [_sb_suffix(z, None, u) for z in zs]
    carry = car[...]
    carries = []
    for rc in rcs:
        carries.append(carry)
        carry = carry + rc[:, LANES:]
    car[...] = carry
    pvs = [weighted(z, rc, c, v_refs[p], None) for p, (z, rc, c) in enumerate(zip(zs, rcs, carries))]
    acc[...] += functools.reduce(lambda a, b: a + b, pvs)

    @pl.when(s == pl.num_programs(1) - 1)
    def _():
        o_ref[...] = acc[...]


def _sb_sample(q, k_new, v_new, cache_k, cache_v, page_table, bias):
    h, b, t, _ = q.shape
    n_phys, page = cache_k.shape[:2]
    n_pages = page_table.shape[1]
    assert page == LANES and t <= page
    rph = _round_up(t, SUBLANES)
    rows = h * rph
    pages = _divisor(n_pages, 8)
    scale = 1.0 / math.sqrt(LANES)
    per_head = lambda a, n: jnp.pad(a.transpose(1, 0, 2, 3), ((0, 0), (0, 0), (0, n - t), (0, 0)))
    q_rows = per_head(q * scale, rph).astype(BF16)
    bias_rows = jnp.broadcast_to(jnp.repeat(bias.astype(F32), rph)[:, None], (rows, LANES))
    tpos = jnp.broadcast_to(jnp.tile(jnp.arange(rph, dtype=jnp.int32), h)[:, None], (rows, LANES))
    tpos = jnp.minimum(tpos, t)
    ck = cache_k.transpose(0, 2, 1, 3)
    cv = cache_v.transpose(0, 2, 1, 3)

    def page_spec(p):
        return pl.BlockSpec((None, h, page, LANES),
                            lambda i, s, pt: (pt[i, n_pages - 1 - (s * pages + p)], 0, 0, 0))

    const = lambda shape: pl.BlockSpec(shape, lambda i, s, pt: (0,) * len(shape))
    per_b = lambda *shape: pl.BlockSpec((None,) + shape, lambda i, s, pt: (i,) + (0,) * len(shape))
    o_rows = pl.pallas_call(
        functools.partial(_sb_sample_kernel, pages=pages),
        grid_spec=pltpu.PrefetchScalarGridSpec(
            num_scalar_prefetch=1,
            grid=(b, n_pages // pages),
            in_specs=[per_b(h, rph, LANES), const((rows, LANES)), const((rows, LANES)),
                      const((page, page + LANES)), per_b(h, page, LANES), per_b(h, page, LANES)]
                     + [page_spec(p) for p in range(pages)] * 2,
            out_specs=per_b(rows, LANES),
            scratch_shapes=[pltpu.VMEM((rows, LANES), F32), pltpu.VMEM((rows, LANES), F32)],
        ),
        out_shape=jax.ShapeDtypeStruct((b, rows, LANES), F32),
        compiler_params=_params("parallel", "arbitrary"),
        name="sb_sample",
    )(page_table, q_rows, bias_rows, tpos, _suffix_matrix(page, totals=True),
      per_head(k_new, page), per_head(v_new, page),
      *([ck] * pages), *([cv] * pages))
    return o_rows.reshape(b, h, rph, LANES)[:, :, :t].transpose(0, 2, 1, 3)


def _dot_hi(a, b):
    return jnp.dot(a, b, precision=HIGHEST, preferred_element_type=F32)


def _split(a):
    hi = a.astype(BF16)
    return hi, (a - hi.astype(F32)).astype(BF16)


def _dot3(a, b):
    return _dot(a[0], b[0]) + _dot(a[0], b[1]) + _dot(a[1], b[0])


def _gdn_prep_kernel(raw_ref, prev_ref, buf_ref, a_ref, b_ref, cw_ref, alog_ref, dtb_ref,
                     u_ref, w_ref, qd_ref, kd_ref, qk_ref, egl_ref, xfull,
                     *, heads, conv_k, chunk, valid_last):
    c = pl.program_id(1)
    rows = raw_ref.shape[0]
    n_chunks = rows // chunk
    gw = heads * LANES

    @pl.when(c == 0)
    def _():
        xfull[0:SUBLANES, :] = buf_ref[...]

    @pl.when(c > 0)
    def _():
        xfull[0:SUBLANES, :] = prev_ref[...]

    xfull[SUBLANES:SUBLANES + rows, :] = raw_ref[...]
    y = None
    for j in range(conv_k):
        start = SUBLANES - (conv_k - 1) + j
        term = xfull[start:start + rows, :] * cw_ref[j:j + 1, :]
        y = term if y is None else y + term
    y = _silu(y)

    g = -jnp.exp(alog_ref[...]) * _softplus(a_ref[...] + dtb_ref[...])
    beta = jax.nn.sigmoid(b_ref[...])
    live = None
    if valid_last < chunk:
        last = c == pl.num_programs(1) - 1
        row = lax.broadcasted_iota(jnp.int32, (rows, 1), 0)
        live_rows = row < jnp.where(last, rows - chunk + valid_last, rows)
        g = jnp.where(live_rows, g, 0.0)
        beta = jnp.where(live_rows, beta, 0.0)
        live = lax.broadcasted_iota(jnp.int32, (chunk, 1), 0) < jnp.where(last, valid_last, chunk)

    shift = int(math.log2(chunk))
    ri = lax.broadcasted_iota(jnp.int32, (rows, rows), 0)
    ci = lax.broadcasted_iota(jnp.int32, (rows, rows), 1)
    same = (ri >> shift) == (ci >> shift)
    gc = _dot_hi(jnp.where(same & (ri >= ci), 1.0, 0.0), g)
    sel = (lax.broadcasted_iota(jnp.int32, (SUBLANES, LANES), 0)
           == lax.broadcasted_iota(jnp.int32, (SUBLANES, LANES), 1)).astype(F32)
    gc_rows = lax.dot_general(sel, gc, _NT, precision=HIGHEST, preferred_element_type=F32)

    ri = lax.broadcasted_iota(jnp.int32, (chunk, chunk), 0)
    ci = lax.broadcasted_iota(jnp.int32, (chunk, chunk), 1)
    incl = ri >= ci
    strict = ri > ci

    for n in range(n_chunks):
        egl_ref[n] = jnp.broadcast_to(jnp.exp(gc[(n + 1) * chunk - 1:(n + 1) * chunk, :]), (SUBLANES, LANES))

    chains = [(n, h) for n in range(n_chunks) for h in range(heads)]
    lows, rhss = [], []
    for n, h in chains:
        rs = slice(n * chunk, (n + 1) * chunk)
        sl = slice(h * LANES, (h + 1) * LANES)
        qh = y[rs, h * LANES:(h + 1) * LANES]
        kh = y[rs, gw + h * LANES:gw + (h + 1) * LANES]
        vh = y[rs, 2 * gw + h * LANES:2 * gw + (h + 1) * LANES]
        qh = qh * lax.rsqrt(jnp.sum(qh * qh, axis=-1, keepdims=True) + NORM_EPS) * (LANES ** -0.5)
        kh = kh * lax.rsqrt(jnp.sum(kh * kh, axis=-1, keepdims=True) + NORM_EPS)
        if live is not None and n == n_chunks - 1:
            qh = jnp.where(live, qh, 0.0)
            kh = jnp.where(live, kh, 0.0)
            vh = jnp.where(live, vh, 0.0)
        g_col = gc[rs, h:h + 1]
        g_row = gc_rows[h:h + 1, rs]
        g_last = gc[(n + 1) * chunk - 1:(n + 1) * chunk, h:h + 1]
        decay = jnp.exp(jnp.where(incl, g_col - g_row, 0.0))
        b_col = beta[rs, h:h + 1]
        eg = jnp.exp(g_col)
        kb = kh * b_col
        k_bf = kh.astype(BF16)
        low = lax.dot_general(kb.astype(BF16), k_bf, _NT, preferred_element_type=F32)
        lows.append(low * jnp.where(strict, decay, 0.0))
        rhss.append(_split(jnp.concatenate([vh * b_col, kb * eg], axis=1)))
        qk = lax.dot_general(qh.astype(BF16), k_bf, _NT, preferred_element_type=F32)
        qk_ref[rs, h * chunk:(h + 1) * chunk] = qk * jnp.where(incl, decay, 0.0)
        qd_ref[rs, sl] = (qh * eg).astype(BF16)
        kd_ref[rs, sl] = (kh * jnp.exp(g_last - g_col)).astype(BF16)

    eye_c = jnp.where(ri == ci, 1.0, 0.0)
    pws = [_split(low) for low in lows]
    invs = [eye_c - low for low in lows]
    for _ in range(shift - 1):
        pws = [_split(_dot3(pw, pw)) for pw in pws]
        invs = [inv + _dot3(_split(inv), pw) for inv, pw in zip(invs, pws)]
    for (n, h), inv, rhs in zip(chains, invs, rhss):
        rs = slice(n * chunk, (n + 1) * chunk)
        sl = slice(h * LANES, (h + 1) * LANES)
        x = _dot3(_split(inv), rhs)
        u_ref[rs, sl] = x[:, :LANES]
        w_ref[rs, sl] = x[:, LANES:].astype(BF16)


def _gdn_scan_kernel(u_ref, w_ref, qd_ref, kd_ref, qk_ref, egl_ref, z_ref, s0_ref, on_ref,
                     o_ref, s_ref, *, heads, chunk):
    bt, rows, _ = u_ref.shape

    @pl.when(pl.program_id(1) == 0)
    def _():
        s_ref[...] = s0_ref[...]

    chains = [(bi, h, slice(h * LANES, (h + 1) * LANES)) for bi in range(bt) for h in range(heads)]
    for n in range(rows // chunk):
        rs = slice(n * chunk, (n + 1) * chunk)
        s_old = [s_ref[bi, h] for bi, h, _ in chains]
        s_bf = [s.astype(BF16) for s in s_old]
        ws = [_dot(w_ref[bi, rs, sl], s) for (bi, _, sl), s in zip(chains, s_bf)]
        qs = [_dot(qd_ref[bi, rs, sl], s) for (bi, _, sl), s in zip(chains, s_bf)]
        v_new = [(u_ref[bi, rs, sl] - x).astype(BF16) for (bi, _, sl), x in zip(chains, ws)]
        os = [x + _dot(qk_ref[bi, rs, h * chunk:(h + 1) * chunk].astype(BF16), v)
              for (bi, h, _), x, v in zip(chains, qs, v_new)]
        upd = [lax.dot_general(kd_ref[bi, rs, sl], v, _TN, preferred_element_type=F32)
               for (bi, _, sl), v in zip(chains, v_new)]
        for (bi, h, sl), s, d, o_h in zip(chains, s_old, upd, os):
            s_ref[bi, h] = s * egl_ref[bi, n, 0:1, h:h + 1] + d
            o_h = _rmsnorm(o_h, on_ref[...]) * _silu(z_ref[bi, rs, sl])
            o_ref[bi, rs, sl] = o_h.astype(o_ref.dtype)


def _gdn(raw, a, b, z, conv_buf, s0, conv_w, a_log, dt_bias, out_norm, *, batch, seq, valid_last):
    heads = s0.shape[1]
    gw = heads * LANES
    conv_k = conv_w.shape[0]
    chunk = GDN_CHUNK
    assert seq % chunk == 0 and heads <= SUBLANES and conv_k - 1 <= SUBLANES
    nc = seq // chunk
    (raw_a, raw_c), (a_a, a_c), (b_a, b_c), (z_a, z_c) = raw, a, b, z
    lane_vec = lambda v: jnp.pad(v.astype(F32), (0, LANES - heads)).reshape(1, LANES)
    buf = jnp.pad(conv_buf, ((0, 0), (SUBLANES - (conv_k - 1), 0), (0, 0)))

    cps = _divisor(nc, 4)
    rows = cps * chunk
    ns = nc // cps
    tok = lambda width, col: pl.BlockSpec((rows, width), lambda i, c: (i * ns + c, col))
    const = lambda shape: pl.BlockSpec(shape, lambda i, c: (0,) * len(shape))
    tokens = lambda width, dtype: jax.ShapeDtypeStruct((batch * seq, width), dtype)
    u, w, qd, kd, qk, egl = pl.pallas_call(
        functools.partial(_gdn_prep_kernel, heads=heads, conv_k=conv_k, chunk=chunk, valid_last=valid_last),
        grid=(batch, ns),
        in_specs=[
            tok(3 * gw, raw_c),
            pl.BlockSpec((SUBLANES, 3 * gw),
                         lambda i, c: (jnp.maximum((i * ns + c) * (rows // SUBLANES) - 1, 0), raw_c)),
            pl.BlockSpec((None, SUBLANES, 3 * gw), lambda i, c: (i, 0, 0)),
            tok(LANES, a_c), tok(LANES, b_c),
            const((conv_k, 3 * gw)), const((1, LANES)), const((1, LANES)),
        ],
        out_specs=[tok(gw, 0), tok(gw, 0), tok(gw, 0), tok(gw, 0), tok(heads * chunk, 0),
                   pl.BlockSpec((cps, SUBLANES, LANES), lambda i, c: (i * ns + c, 0, 0))],
        out_shape=[tokens(gw, F32), tokens(gw, BF16), tokens(gw, BF16), tokens(gw, BF16),
                   tokens(heads * chunk, F32),
                   jax.ShapeDtypeStruct((batch * nc, SUBLANES, LANES), F32)],
        scratch_shapes=[pltpu.VMEM((SUBLANES + rows, 3 * gw), F32)],
        compiler_params=_params("parallel", "arbitrary"),
        name="gdn_prep",
    )(raw_a, raw_a, buf, a_a, b_a, conv_w, lane_vec(a_log), lane_vec(dt_bias))

    bt = 2 if batch % 2 == 0 else 1
    nz = z_a.shape[1]
    seq3 = lambda arr: arr.reshape(batch, seq, arr.shape[-1])
    tok3 = lambda width, col=0: pl.BlockSpec((bt, rows, width), lambda i, c: (i, c, col))
    state = pl.BlockSpec((bt, heads, LANES, LANES), lambda i, c: (i, 0, 0, 0))
    o, s_new = pl.pallas_call(
        functools.partial(_gdn_scan_kernel, heads=heads, chunk=chunk),
        grid=(batch // bt, ns),
        in_specs=[tok3(gw), tok3(gw), tok3(gw), tok3(gw), tok3(heads * chunk),
                  pl.BlockSpec((bt, cps, SUBLANES, LANES), lambda i, c: (i, c, 0, 0)),
                  tok3(gw, z_c), state, const((1, LANES))],
        out_specs=[tok3(gw), state],
        out_shape=[jax.ShapeDtypeStruct((batch, seq, gw), BF16), jax.ShapeDtypeStruct(s0.shape, F32)],
        compiler_params=_params("parallel", "arbitrary"),
        name="gdn_scan",
    )(seq3(u), seq3(w), seq3(qd), seq3(kd), seq3(qk), egl.reshape(batch, nc, SUBLANES, LANES),
      z_a.reshape(batch, seq, nz), s0.astype(F32), out_norm.astype(F32).reshape(1, LANES))
    return o.reshape(batch * seq, gw), s_new


def _mem_attn_kernel(q_ref, k_ref, v_ref, o_ref, *, scale):
    for h in range(k_ref.shape[0]):
        sl = slice(h * LANES, (h + 1) * LANES)
        q = (q_ref[:, sl] * scale).astype(BF16)
        z = lax.dot_general(q, k_ref[h].astype(BF16), _NT, preferred_element_type=F32)
        e = jnp.exp(z - jnp.max(z, axis=-1, keepdims=True))
        p = e / jnp.sum(e, axis=-1, keepdims=True)
        o_ref[:, sl] = _dot(p.astype(BF16), v_ref[h].astype(BF16)).astype(o_ref.dtype)


def _mem_attn(q, mem_k, mem_v, *, batch, seq):
    q_a, q_c = q
    _, heads, n_mem, _ = mem_k.shape
    xw = heads * LANES
    tq = _pick(seq, 512)
    nq = seq // tq
    mem_spec = pl.BlockSpec((None, heads, n_mem, LANES), lambda i, j: (i, 0, 0, 0))
    return pl.pallas_call(
        functools.partial(_mem_attn_kernel, scale=1.0 / math.sqrt(LANES)),
        grid=(batch, nq),
        in_specs=[pl.BlockSpec((tq, xw), lambda i, j: (i * nq + j, q_c)), mem_spec, mem_spec],
        out_specs=pl.BlockSpec((tq, xw), lambda i, j: (i * nq + j, 0)),
        out_shape=jax.ShapeDtypeStruct((batch * seq, xw), BF16),
        compiler_params=_params("parallel", "parallel"),
        name="mem_attn",
    )(q_a, mem_k, mem_v)


def _merge_kernel(x_ref, osb_ref, ogdn_ref, ox_ref, gsb_ref, ggdn_ref, gx_ref,
                  wsb_ref, wgdn_ref, wx_ref, wout_ref, o_ref):
    merged = (jax.nn.sigmoid(gsb_ref[...]) * _dot(osb_ref[...], wsb_ref[...])
              + jax.nn.sigmoid(ggdn_ref[...]) * _dot(ogdn_ref[...], wgdn_ref[...])
              + jax.nn.sigmoid(gx_ref[...]) * _dot(ox_ref[...], wx_ref[...]))
    o_ref[...] = x_ref[...] + _dot(merged.astype(BF16), wout_ref[...])


def _merge(x, o_sb, o_gdn, o_x, proj, gate_col, w_up_sb, w_up_gdn, w_up_x, w_out):
    t, d = x.shape
    tm = _pick(t, 256)
    tok = lambda a: pl.BlockSpec((tm, a.shape[1]), lambda i: (i, 0))
    gate = lambda n: pl.BlockSpec((tm, d), lambda i: (i, gate_col + n))
    weight = lambda w: pl.BlockSpec(w.shape, lambda i: (0, 0), pipeline_mode=pl.Buffered(1))
    return pl.pallas_call(
        _merge_kernel,
        grid=(t // tm,),
        in_specs=[tok(x), tok(o_sb), tok(o_gdn), tok(o_x), gate(0), gate(1), gate(2),
                  weight(w_up_sb), weight(w_up_gdn), weight(w_up_x), weight(w_out)],
        out_specs=tok(x),
        out_shape=jax.ShapeDtypeStruct((t, d), F32),
        compiler_params=_params("parallel"),
        name="merge",
    )(x, o_sb, o_gdn, o_x, proj, proj, proj, w_up_sb, w_up_gdn, w_up_x, w_out)


def _split_w_in(w_in, sbw, gw, xw, d, g_heads):
    w_sb = w_in[:, :3 * sbw]
    o = 3 * sbw
    w_gqkv = w_in[:, o:o + 3 * gw]
    o += 3 * gw
    w_a = w_in[:, o:o + g_heads]
    w_b = w_in[:, o + g_heads:o + 2 * g_heads]
    o += 2 * g_heads
    w_z = w_in[:, o:o + gw]
    w_xq = w_in[:, o + gw:o + gw + xw]
    gates = w_in[:, o + gw + xw:]
    assert gates.shape[1] == 3 * d
    pad = lambda w: jnp.pad(w, ((0, 0), (0, LANES - g_heads)))
    rest = jnp.concatenate([gates, w_z, w_gqkv, w_xq, pad(w_a), pad(w_b)], axis=1)
    return w_sb.astype(BF16), rest.astype(BF16)


def kernel(x_prompt, x_sample, cache_sb_k, cache_sb_v, cache_mem_k, cache_mem_v, state_gdn_S, state_gdn_conv, page_table, mem_prompt, ffn1_norm, ffn1_w_in, ffn1_w_out, mix_norm, w_in, sb_logit_bias, gdn_conv_w, gdn_a_log, gdn_dt_bias, gdn_out_norm, mem_norm, w_mem_kv, w_up_sb, w_up_gdn, w_up_x, w_out, ffn2_norm, ffn2_w_in, ffn2_w_out, final_norm):
    batch, seq, d = x_prompt.shape
    dec_batch, dec_seq, _ = x_sample.shape
    depth = w_in.shape[0]
    sb_heads = cache_sb_k.shape[3]
    x_heads = cache_mem_k.shape[3]
    g_heads = state_gdn_S.shape[2]
    n_mem = mem_prompt.shape[1]
    conv_k = gdn_conv_w.shape[1]
    sbw, gw, xw = sb_heads * LANES, g_heads * LANES, x_heads * LANES

    off_gate, off_z = 0, 3 * d
    off_gdn = off_z + gw
    off_xq = off_gdn + 3 * gw
    off_a, off_b = off_xq + xw, off_xq + xw + LANES
    n_proj = off_b + LANES
    assert off_z % gw == 0 and off_gdn % (3 * gw) == 0 and off_xq % xw == 0
    tn_proj = 3 * 256
    assert n_proj % tn_proj == 0

    xp = x_prompt.reshape(batch * seq, d)
    xs = x_sample.reshape(dec_batch * dec_seq, d)
    outs = [[] for _ in range(10)]
    for l in range(depth):
        bf = lambda w: w[l].astype(BF16)
        w1_in, w1_out, w2_in, w2_out = bf(ffn1_w_in), bf(ffn1_w_out), bf(ffn2_w_in), bf(ffn2_w_out)
        w_sb, w_rest = _split_w_in(w_in[l], sbw, gw, xw, d, g_heads)
        wup_sb, wup_gdn, wup_x, wo = bf(w_up_sb), bf(w_up_gdn), bf(w_up_x), bf(w_out)
        last = l == depth - 1
        mixer_w = (gdn_conv_w[l], gdn_a_log[l], gdn_dt_bias[l], gdn_out_norm[l])

        def mix_tail(x, proj, o_sb, o_gdn, o_x):
            x = _merge(x, o_sb, o_gdn, o_x, proj, off_gate // d, wup_sb, wup_gdn, wup_x, wo)
            return _ffn(x, ffn2_norm[l], w2_in, w2_out, final_norm, final=last)

        mk_p, mv_p = _normed_matmul_heads(mem_prompt.reshape(batch * n_mem, d), mem_norm[l], bf(w_mem_kv),
                                          sections=2, batch=batch, seq=n_mem, name="mem_kv")
        xp = _ffn(xp, ffn1_norm[l], w1_in, w1_out, final_norm, final=False)
        q_p, k_p, v_p = _normed_matmul_heads(xp, mix_norm[l], w_sb, sections=3, batch=batch, seq=seq,
                                             name="sb_proj")
        proj = _normed_matmul(xp, mix_norm[l], w_rest, tn=tn_proj, name="mixer_proj")
        o_sb = _sb_prompt(q_p, k_p, v_p, sb_logit_bias[l].astype(F32))
        o_gdn, s_p = _gdn((proj, off_gdn // (3 * gw)), (proj, off_a // LANES), (proj, off_b // LANES),
                          (proj, off_z // gw), jnp.zeros((batch, conv_k - 1, 3 * gw), F32),
                          jnp.zeros((batch, g_heads, LANES, LANES), F32), *mixer_w,
                          batch=batch, seq=seq, valid_last=GDN_CHUNK)
        o_x = _mem_attn((proj, off_xq // xw), mk_p, mv_p, batch=batch, seq=seq)
        xp = mix_tail(xp, proj, o_sb, o_gdn, o_x)
        seq_major = lambda a: a.transpose(0, 2, 1, 3)
        outs[0].append(seq_major(k_p))
        outs[1].append(seq_major(v_p))
        outs[2].append(seq_major(mk_p))
        outs[3].append(seq_major(mv_p))
        outs[4].append(s_p)
        assert seq >= conv_k - 1
        outs[5].append(proj.reshape(batch, seq, n_proj)[:, seq - (conv_k - 1):, off_gdn:off_gdn + 3 * gw])

        xs = _ffn(xs, ffn1_norm[l], w1_in, w1_out, final_norm, final=False)
        qkv = _normed_matmul_heads(xs, mix_norm[l], w_sb, sections=3, batch=1, seq=dec_batch * dec_seq,
                                   name="sb_proj_sample")
        proj = _normed_matmul(xs, mix_norm[l], w_rest, tn=tn_proj, name="mixer_proj_sample")
        q_s, k_s, v_s = (a.reshape(sb_heads, dec_batch, dec_seq, LANES) for a in qkv)
        o_sb = _sb_sample(q_s, k_s, v_s, cache_sb_k[l], cache_sb_v[l], page_table, sb_logit_bias[l])
        o_sb = o_sb.reshape(dec_batch * dec_seq, sbw).astype(BF16)
        proj3 = proj.reshape(dec_batch, dec_seq, n_proj)
        seq_g = _round_up(dec_seq, GDN_CHUNK)
        seq_x = _round_up(dec_seq, SUBLANES)
        padded = lambda lo, hi, n: jnp.pad(proj3[:, :, lo:hi], ((0, 0), (0, n - dec_seq), (0, 0))
                                           ).reshape(dec_batch * n, hi - lo)
        raw_s = proj3[:, :, off_gdn:off_gdn + 3 * gw]
        o_gdn, s_s = _gdn((padded(off_gdn, off_gdn + 3 * gw, seq_g), 0), (padded(off_a, off_a + LANES, seq_g), 0),
                          (padded(off_b, off_b + LANES, seq_g), 0), (padded(off_z, off_z + gw, seq_g), 0),
                          state_gdn_conv[l], state_gdn_S[l], *mixer_w,
                          batch=dec_batch, seq=seq_g, valid_last=dec_seq - (seq_g - GDN_CHUNK))
        o_gdn = o_gdn.reshape(dec_batch, seq_g, gw)[:, :dec_seq].reshape(dec_batch * dec_seq, gw)
        o_x = _mem_attn((padded(off_xq, off_xq + xw, seq_x), 0), cache_mem_k[l].transpose(0, 2, 1, 3),
                        cache_mem_v[l].transpose(0, 2, 1, 3), batch=dec_batch, seq=seq_x)
        o_x = o_x.reshape(dec_batch, seq_x, xw)[:, :dec_seq].reshape(dec_batch * dec_seq, xw)
        xs = mix_tail(xs, proj, o_sb, o_gdn, o_x)
        outs[6].append(k_s.transpose(1, 2, 0, 3))
        outs[7].append(v_s.transpose(1, 2, 0, 3))
        outs[8].append(s_s)
        conv_all = jnp.concatenate([state_gdn_conv[l].astype(F32), raw_s], axis=1)
        outs[9].append(conv_all[:, dec_seq:])

    stack = [jnp.stack(o) for o in outs]
    return (xp.reshape(batch, seq, d), xs.reshape(dec_batch, dec_seq, d), *stack)
```

```python
import functools
import math

import jax
import jax.numpy as jnp
from jax import lax
from jax.experimental import pallas as pl
from jax.experimental.pallas import tpu as pltpu

F32 = jnp.float32
BF16 = jnp.bfloat16
NORM_EPS = 1e-6
LANES = 128
SUBLANES = 8
GDN_CHUNK = 64
VMEM_LIMIT = 56 * 1024 * 1024
HIGHEST = lax.Precision.HIGHEST

_NT = (((1,), (1,)), ((), ()))
_TN = (((0,), (0,)), ((), ()))


def _params(*sem):
    return pltpu.CompilerParams(dimension_semantics=sem, vmem_limit_bytes=VMEM_LIMIT)


def _dot(a, b):
    return jnp.dot(a, b, preferred_element_type=F32)


def _rmsnorm(x, g):
    return x * lax.rsqrt(jnp.mean(x * x, axis=-1, keepdims=True) + NORM_EPS) * g


def _softplus(x):
    return jnp.maximum(x, 0.0) + jnp.log(1.0 + jnp.exp(-jnp.abs(x)))


def _silu(x):
    return x * jax.nn.sigmoid(x)


def _pick(n, pref):
    if n <= pref:
        return n
    t = pref
    while n % t:
        t //= 2
    assert t >= SUBLANES, (n, pref)
    return t


def _divisor(n, pref):
    return max(k for k in range(1, min(n, pref) + 1) if n % k == 0)


def _round_up(n, m):
    return -(-n // m) * m


def _ffn_kernel(x_ref, g_ref, wg_ref, wu_ref, wo_ref, fg_ref, o_ref, xn_ref, *, final):
    f = pl.program_id(1)

    @pl.when(f == 0)
    def _():
        xn_ref[...] = _rmsnorm(x_ref[...], g_ref[...]).astype(BF16)
        o_ref[...] = jnp.zeros_like(o_ref)

    xn = xn_ref[...]
    h = (_silu(_dot(xn, wg_ref[...])) * _dot(xn, wu_ref[...])).astype(BF16)
    o_ref[...] += _dot(h, wo_ref[...])

    @pl.when(f == pl.num_programs(1) - 1)
    def _():
        y = x_ref[...] + 0.5 * o_ref[...]
        if final:
            y = _rmsnorm(y, fg_ref[...])
        o_ref[...] = y


def _ffn(x, g, w_in, w_out, final_g, *, final):
    t, d = x.shape
    dff = w_out.shape[0]
    tm = _pick(t, 512)
    tf = _pick(dff, 512)
    nf = dff // tf
    return pl.pallas_call(
        functools.partial(_ffn_kernel, final=final),
        grid=(t // tm, nf),
        in_specs=[
            pl.BlockSpec((tm, d), lambda i, f: (i, 0)),
            pl.BlockSpec((1, d), lambda i, f: (0, 0)),
            pl.BlockSpec((d, tf), lambda i, f: (0, f)),
            pl.BlockSpec((d, tf), lambda i, f: (0, f + nf)),
            pl.BlockSpec((tf, d), lambda i, f: (f, 0)),
            pl.BlockSpec((1, d), lambda i, f: (0, 0)),
        ],
        out_specs=pl.BlockSpec((tm, d), lambda i, f: (i, 0)),
        out_shape=jax.ShapeDtypeStruct((t, d), F32),
        scratch_shapes=[pltpu.VMEM((tm, d), BF16)],
        compiler_params=_params("parallel", "arbitrary"),
        name="ffn_final" if final else "ffn",
    )(x, g.reshape(1, d), w_in, w_in, w_out, final_g.reshape(1, d))


def _nmm_kernel(x_ref, g_ref, w_ref, *rest):
    *o_refs, xn_ref = rest
    j = pl.program_id(1)

    @pl.when(j == 0)
    def _():
        xn_ref[...] = _rmsnorm(x_ref[...], g_ref[...]).astype(BF16)

    y = _dot(xn_ref[...], w_ref[...])
    if len(o_refs) == 1 and len(o_refs[0].shape) == 2:
        o_refs[0][...] = y
        return
    for s, o_ref in enumerate(o_refs):
        @pl.when(j == s)
        def _():
            for n in range(o_ref.shape[0]):
                o_ref[n] = y[:, n * LANES:(n + 1) * LANES]


def _normed_matmul(x, g, w, *, tn, name):
    t, d = x.shape
    n = w.shape[1]
    assert n % tn == 0 and tn % LANES == 0
    tm = _pick(t, 1024)
    return pl.pallas_call(
        _nmm_kernel,
        grid=(t // tm, n // tn),
        in_specs=[
            pl.BlockSpec((tm, d), lambda i, j: (i, 0)),
            pl.BlockSpec((1, d), lambda i, j: (0, 0)),
            pl.BlockSpec((d, tn), lambda i, j: (0, j)),
        ],
        out_specs=pl.BlockSpec((tm, tn), lambda i, j: (i, j)),
        out_shape=jax.ShapeDtypeStruct((t, n), F32),
        scratch_shapes=[pltpu.VMEM((tm, d), BF16)],
        compiler_params=_params("parallel", "arbitrary"),
        name=name,
    )(x, g.reshape(1, d), w)


def _normed_matmul_heads(x, g, w, *, sections, batch, seq, name):
    t, d = x.shape
    tn = w.shape[1] // sections
    heads = tn // LANES
    assert t == batch * seq and tn % LANES == 0
    tm = _pick(seq, 1024)
    nt = seq // tm
    out_spec = pl.BlockSpec((None, heads, tm, LANES), lambda i, j: (i // nt, 0, i % nt, 0))
    return pl.pallas_call(
        _nmm_kernel,
        grid=(t // tm, sections),
        in_specs=[
            pl.BlockSpec((tm, d), lambda i, j: (i, 0)),
            pl.BlockSpec((1, d), lambda i, j: (0, 0)),
            pl.BlockSpec((d, tn), lambda i, j: (0, j)),
        ],
        out_specs=[out_spec] * sections,
        out_shape=[jax.ShapeDtypeStruct((batch, heads, seq, LANES), F32)] * sections,
        scratch_shapes=[pltpu.VMEM((tm, d), BF16)],
        compiler_params=_params("parallel", "arbitrary"),
        name=name,
    )(x, g.reshape(1, d), w)


LOG2_E = math.log2(math.e)


def _sb_suffix(z, valid, u):
    sp = jnp.maximum(z, 0.0) + jnp.log(1.0 + jnp.exp2(jnp.abs(z) * -LOG2_E))
    if valid is not None:
        sp = jnp.where(valid, sp, 0.0)
    return _dot(sp.astype(BF16), u)


def _sb_exp(z, r, valid):
    w = jnp.exp(z - r)
    if valid is not None:
        w = jnp.where(valid, w, 0.0)
    return w.astype(BF16)


def _suffix_matrix(w_keys, totals=False):
    n = w_keys + (LANES if totals else 0)
    j = lax.broadcasted_iota(jnp.int32, (w_keys, n), 0)
    s = lax.broadcasted_iota(jnp.int32, (w_keys, n), 1)
    return ((j >= s) | (s >= w_keys)).astype(BF16)


def _sb_prompt_kernel(bias_ref, q_ref, k_ref, v_ref, u_ref, o_ref, kbf, vbf, acc, car, *, scale, blk, unroll):
    h = pl.program_id(1)
    i = pl.program_id(2)

    @pl.when(i == 0)
    def _():
        kbf[...] = k_ref[...].astype(BF16)
        vbf[...] = v_ref[...].astype(BF16)

    q = (q_ref[...] * scale).astype(BF16)
    bias = bias_ref[h]
    u = u_ref[...]
    rows = lax.broadcasted_iota(jnp.int32, (blk, blk), 0)
    cols = lax.broadcasted_iota(jnp.int32, (blk, blk), 1)

    def blocks(js, diagonal, carry):
        offs = [pl.multiple_of(j * blk, blk) for j in js]
        valids = [cols < rows if diagonal and n == 0 else None for n in range(len(js))]
        zs = [lax.dot_general(q, kbf[pl.ds(off, blk), :], _NT, preferred_element_type=F32) + bias
              for off in offs]
        sufs = [_sb_suffix(z, valid, u) for z, valid in zip(zs, valids)]
        pv = None
        for off, z, suf, valid in zip(offs, zs, sufs, valids):
            r = suf + jnp.concatenate([carry] * (blk // LANES), axis=1)
            carry = jnp.broadcast_to(r[:, 0:1], carry.shape)
            term = _dot(_sb_exp(z, r, valid), vbf[pl.ds(off, blk), :])
            pv = term if pv is None else pv + term
        return pv, carry

    for r in range(2):
        @pl.when(i % 2 == r)
        def _():
            pv, c = blocks([i - m for m in range(r + 1)], True, jnp.zeros((blk, LANES), F32))
            acc[...] = pv
            car[...] = c

    def group(j, size):
        pv, c = blocks([j - m for m in range(size)], False, car[...])
        acc[...] += pv
        car[...] = c

    j = i - 1 - i % 2
    size = 2
    while size < unroll:
        bit = (i // size) % 2

        @pl.when(bit == 1)
        def _():
            group(j, size)

        j = j - size * bit
        size *= 2

    def body(n, _):
        group(j - unroll * n, unroll)
        return 0

    lax.fori_loop(0, i // unroll, body, 0)
    o_ref[...] = acc[...].astype(o_ref.dtype)


def _sb_prompt(q, k, v, bias):
    batch, heads, seq, _ = q.shape
    blk = _pick(seq, 256)
    assert blk % LANES == 0 and seq % blk == 0
    nq = seq // blk
    whole = pl.BlockSpec((None, None, seq, LANES), lambda b, h, i: (b, h, 0, 0))
    return pl.pallas_call(
        functools.partial(_sb_prompt_kernel, scale=1.0 / math.sqrt(LANES), blk=blk, unroll=16),
        grid=(batch, heads, nq),
        in_specs=[
            pl.BlockSpec(memory_space=pltpu.SMEM),
            pl.BlockSpec((None, None, blk, LANES), lambda b, h, i: (b, h, i, 0)),
            whole, whole,
            pl.BlockSpec((blk, blk), lambda b, h, i: (0, 0)),
        ],
        out_specs=pl.BlockSpec((blk, LANES), lambda b, h, i: (b * nq + i, h)),
        out_shape=jax.ShapeDtypeStruct((batch * seq, heads * LANES), BF16),
        scratch_shapes=[
            pltpu.VMEM((seq, LANES), BF16),
            pltpu.VMEM((seq, LANES), BF16),
            pltpu.VMEM((blk, LANES), F32),
            pltpu.VMEM((blk, LANES), F32),
        ],
        compiler_params=_params("parallel", "parallel", "arbitrary"),
        name="sb_prompt",
    )(bias, q, k, v, _suffix_matrix(blk))


def _sb_sample_kernel(pt_ref, q_ref, bias_ref, tpos_ref, u_ref, kn_ref, vn_ref, *rest, pages):
    k_refs = rest[:pages]
    v_refs = rest[pages:2 * pages]
    o_ref, acc, car = rest[2 * pages:]
    s = pl.program_id(1)
    heads, rph, _ = q_ref.shape
    bias = bias_ref[...]
    u = u_ref[...]

    def logits(k_ref):
        return jnp.concatenate(
            [lax.dot_general(q_ref[h], k_ref[h].astype(BF16), _NT, preferred_element_type=F32)
             for h in range(heads)], axis=0) + bias

    def weighted(z, rc, carry, v_ref, valid):
        w = _sb_exp(z, rc[:, :LANES] + carry, valid)
        return jnp.concatenate(
            [_dot(w[h * rph:(h + 1) * rph], v_ref[h].astype(BF16)) for h in range(heads)], axis=0)

    @pl.when(s == 0)
    def _():
        valid = lax.broadcasted_iota(jnp.int32, tpos_ref.shape, 1) < tpos_ref[...]
        z = logits(kn_ref)
        rc = _sb_suffix(z, valid, u)
        acc[...] = weighted(z, rc, jnp.zeros_like(z), vn_ref, valid)
        car[...] = rc[:, LANES:]

    zs = [logits(k_refs[p]) for p in range(pages)]
    rcs = [_sb_suffix(z, None, u) for z in zs]
    carry = car[...]
    carries = []
    for rc in rcs:
        carries.append(carry)
        carry = carry + rc[:, LANES:]
    car[...] = carry
    pvs = [weighted(z, rc, c, v_refs[p], None) for p, (z, rc, c) in enumerate(zip(zs, rcs, carries))]
    acc[...] += functools.reduce(lambda a, b: a + b, pvs)

    @pl.when(s == pl.num_programs(1) - 1)
    def _():
        o_ref[...] = acc[...]


def _sb_sample(q, k_new, v_new, cache_k, cache_v, page_table, bias):
    h, b, t, _ = q.shape
    n_phys, page = cache_k.shape[:2]
    n_pages = page_table.shape[1]
    assert page == LANES and t <= page
    rph = _round_up(t, SUBLANES)
    rows = h * rph
    pages = _divisor(n_pages, 8)
    scale = 1.0 / math.sqrt(LANES)
    per_head = lambda a, n: jnp.pad(a.transpose(1, 0, 2, 3), ((0, 0), (0, 0), (0, n - t), (0, 0)))
    q_rows = per_head(q * scale, rph).astype(BF16)
    bias_rows = jnp.broadcast_to(jnp.repeat(bias.astype(F32), rph)[:, None], (rows, LANES))
    tpos = jnp.broadcast_to(jnp.tile(jnp.arange(rph, dtype=jnp.int32), h)[:, None], (rows, LANES))
    tpos = jnp.minimum(tpos, t)
    ck = cache_k.transpose(0, 2, 1, 3)
    cv = cache_v.transpose(0, 2, 1, 3)

    def page_spec(p):
        return pl.BlockSpec((None, h, page, LANES),
                            lambda i, s, pt: (pt[i, n_pages - 1 - (s * pages + p)], 0, 0, 0))

    const = lambda shape: pl.BlockSpec(shape, lambda i, s, pt: (0,) * len(shape))
    per_b = lambda *shape: pl.BlockSpec((None,) + shape, lambda i, s, pt: (i,) + (0,) * len(shape))
    o_rows = pl.pallas_call(
        functools.partial(_sb_sample_kernel, pages=pages),
        grid_spec=pltpu.PrefetchScalarGridSpec(
            num_scalar_prefetch=1,
            grid=(b, n_pages // pages),
            in_specs=[per_b(h, rph, LANES), const((rows, LANES)), const((rows, LANES)),
                      const((page, page + LANES)), per_b(h, page, LANES), per_b(h, page, LANES)]
                     + [page_spec(p) for p in range(pages)] * 2,
            out_specs=per_b(rows, LANES),
            scratch_shapes=[pltpu.VMEM((rows, LANES), F32), pltpu.VMEM((rows, LANES), F32)],
        ),
        out_shape=jax.ShapeDtypeStruct((b, rows, LANES), F32),
        compiler_params=_params("parallel", "arbitrary"),
        name="sb_sample",
    )(page_table, q_rows, bias_rows, tpos, _suffix_matrix(page, totals=True),
      per_head(k_new, page), per_head(v_new, page),
      *([ck] * pages), *([cv] * pages))
    return o_rows.reshape(b, h, rph, LANES)[:, :, :t].transpose(0, 2, 1, 3)


def _dot_hi(a, b):
    return jnp.dot(a, b, precision=HIGHEST, preferred_element_type=F32)


def _split(a):
    hi = a.astype(BF16)
    return hi, (a - hi.astype(F32)).astype(BF16)


def _dot3(a, b):
    return _dot(a[0], b[0]) + _dot(a[0], b[1]) + _dot(a[1], b[0])


def _gdn_prep_kernel(raw_ref, prev_ref, buf_ref, a_ref, b_ref, cw_ref, alog_ref, dtb_ref,
                     u_ref, w_ref, qd_ref, kd_ref, qk_ref, egl_ref, xfull,
                     *, heads, conv_k, chunk, valid_last):
    c = pl.program_id(1)
    rows = raw_ref.shape[0]
    n_chunks = rows // chunk
    gw = heads * LANES

    @pl.when(c == 0)
    def _():
        xfull[0:SUBLANES, :] = buf_ref[...]

    @pl.when(c > 0)
    def _():
        xfull[0:SUBLANES, :] = prev_ref[...]

    xfull[SUBLANES:SUBLANES + rows, :] = raw_ref[...]
    x = xfull[...]
    y = x[SUBLANES:] * cw_ref[conv_k - 1:conv_k, :]
    for j in range(conv_k - 1):
        back = pltpu.roll(x, conv_k - 1 - j, axis=0)
        y = y + back[SUBLANES:] * cw_ref[j:j + 1, :]
    y = _silu(y)

    ones = jnp.ones((LANES, LANES), BF16)

    def l2norm(a):
        ss = _dot((a * a).astype(BF16), ones)
        return a * lax.rsqrt(ss + NORM_EPS)

    qn = [l2norm(y[:, h * LANES:(h + 1) * LANES]) * (LANES ** -0.5) for h in range(heads)]
    kn = [l2norm(y[:, gw + h * LANES:gw + (h + 1) * LANES]) for h in range(heads)]

    g = -jnp.exp(alog_ref[...]) * _softplus(a_ref[...] + dtb_ref[...])
    beta = jax.nn.sigmoid(b_ref[...])
    live = None
    if valid_last < chunk:
        last = c == pl.num_programs(1) - 1
        row = lax.broadcasted_iota(jnp.int32, (rows, 1), 0)
        live_rows = row < jnp.where(last, rows - chunk + valid_last, rows)
        g = jnp.where(live_rows, g, 0.0)
        beta = jnp.where(live_rows, beta, 0.0)
        live = lax.broadcasted_iota(jnp.int32, (chunk, 1), 0) < jnp.where(last, valid_last, chunk)

    shift = int(math.log2(chunk))
    ri = lax.broadcasted_iota(jnp.int32, (rows, rows), 0)
    ci = lax.broadcasted_iota(jnp.int32, (rows, rows), 1)
    same = (ri >> shift) == (ci >> shift)
    gc = _dot_hi(jnp.where(same & (ri >= ci), 1.0, 0.0), g)
    sel = (lax.broadcasted_iota(jnp.int32, (SUBLANES, LANES), 0)
           == lax.broadcasted_iota(jnp.int32, (SUBLANES, LANES), 1)).astype(F32)
    gc_rows = lax.dot_general(sel, gc, _NT, precision=HIGHEST, preferred_element_type=F32)

    ri = lax.broadcasted_iota(jnp.int32, (chunk, chunk), 0)
    ci = lax.broadcasted_iota(jnp.int32, (chunk, chunk), 1)
    incl = ri >= ci
    strict = ri > ci

    for n in range(n_chunks):
        egl_ref[n] = jnp.broadcast_to(jnp.exp(gc[(n + 1) * chunk - 1:(n + 1) * chunk, :]), (SUBLANES, LANES))

    chains = [(n, h) for n in range(n_chunks) for h in range(heads)]
    lows, rhss = [], []
    for n, h in chains:
        rs = slice(n * chunk, (n + 1) * chunk)
        sl = slice(h * LANES, (h + 1) * LANES)
        qh = qn[h][rs]
        kh = kn[h][rs]
        vh = y[rs, 2 * gw + h * LANES:2 * gw + (h + 1) * LANES]
        if live is not None and n == n_chunks - 1:
            qh = jnp.where(live, qh, 0.0)
            kh = jnp.where(live, kh, 0.0)
            vh = jnp.where(live, vh, 0.0)
        g_col = gc[rs, h:h + 1]
        g_row = gc_rows[h:h + 1, rs]
        g_last = gc[(n + 1) * chunk - 1:(n + 1) * chunk, h:h + 1]
        decay = jnp.exp(jnp.where(incl, g_col - g_row, 0.0))
        b_col = beta[rs, h:h + 1]
        eg = jnp.exp(g_col)
        kb = kh * b_col
        k_bf = kh.astype(BF16)
        low = lax.dot_general(kb.astype(BF16), k_bf, _NT, preferred_element_type=F32)
        lows.append(low * jnp.where(strict, decay, 0.0))
        rhss.append(_split(jnp.concatenate([vh * b_col, kb * eg], axis=1)))
        qk = lax.dot_general(qh.astype(BF16), k_bf, _NT, preferred_element_type=F32)
        qk_ref[rs, h * chunk:(h + 1) * chunk] = qk * jnp.where(incl, decay, 0.0)
        qd_ref[rs, sl] = (qh * eg).astype(BF16)
        kd_ref[rs, sl] = (kh * jnp.exp(g_last - g_col)).astype(BF16)

    eye_c = jnp.where(ri == ci, 1.0, 0.0)
    pws = [_split(low) for low in lows]
    invs = [eye_c - low for low in lows]
    for _ in range(shift - 1):
        pws = [_split(_dot3(pw, pw)) for pw in pws]
        invs = [inv + _dot3(_split(inv), pw) for inv, pw in zip(invs, pws)]
    for (n, h), inv, rhs in zip(chains, invs, rhss):
        rs = slice(n * chunk, (n + 1) * chunk)
        sl = slice(h * LANES, (h + 1) * LANES)
        x = _dot3(_split(inv), rhs)
        u_ref[rs, sl] = x[:, :LANES]
        w_ref[rs, sl] = x[:, LANES:].astype(BF16)


def _gdn_scan_kernel(u_ref, w_ref, qd_ref, kd_ref, qk_ref, egl_ref, z_ref, s0_ref, on_ref,
                     o_ref, s_ref, *, heads, chunk):
    bt, rows, _ = u_ref.shape

    @pl.when(pl.program_id(1) == 0)
    def _():
        s_ref[...] = s0_ref[...]

    chains = [(bi, h, slice(h * LANES, (h + 1) * LANES)) for bi in range(bt) for h in range(heads)]
    for n in range(rows // chunk):
        rs = slice(n * chunk, (n + 1) * chunk)
        s_old = [s_ref[bi, h] for bi, h, _ in chains]
        s_bf = [s.astype(BF16) for s in s_old]
        ws = [_dot(w_ref[bi, rs, sl], s) for (bi, _, sl), s in zip(chains, s_bf)]
        qs = [_dot(qd_ref[bi, rs, sl], s) for (bi, _, sl), s in zip(chains, s_bf)]
        v_new = [(u_ref[bi, rs, sl] - x).astype(BF16) for (bi, _, sl), x in zip(chains, ws)]
        os = [x + _dot(qk_ref[bi, rs, h * chunk:(h + 1) * chunk].astype(BF16), v)
              for (bi, h, _), x, v in zip(chains, qs, v_new)]
        upd = [lax.dot_general(kd_ref[bi, rs, sl], v, _TN, preferred_element_type=F32)
               for (bi, _, sl), v in zip(chains, v_new)]
        for (bi, h, sl), s, d, o_h in zip(chains, s_old, upd, os):
            s_ref[bi, h] = s * egl_ref[bi, n, 0:1, h:h + 1] + d
            o_h = _rmsnorm(o_h, on_ref[...]) * _silu(z_ref[bi, rs, sl])
            o_ref[bi, rs, sl] = o_h.astype(o_ref.dtype)


def _gdn(raw, a, b, z, conv_buf, s0, conv_w, a_log, dt_bias, out_norm, *, batch, seq, valid_last):
    heads = s0.shape[1]
    gw = heads * LANES
    conv_k = conv_w.shape[0]
    chunk = GDN_CHUNK
    assert seq % chunk == 0 and heads <= SUBLANES and conv_k - 1 <= SUBLANES
    nc = seq // chunk
    (raw_a, raw_c), (a_a, a_c), (b_a, b_c), (z_a, z_c) = raw, a, b, z
    lane_vec = lambda v: jnp.pad(v.astype(F32), (0, LANES - heads)).reshape(1, LANES)
    buf = jnp.pad(conv_buf, ((0, 0), (SUBLANES - (conv_k - 1), 0), (0, 0)))

    cps = _divisor(nc, 4)
    rows = cps * chunk
    ns = nc // cps
    tok = lambda width, col: pl.BlockSpec((rows, width), lambda i, c: (i * ns + c, col))
    const = lambda shape: pl.BlockSpec(shape, lambda i, c: (0,) * len(shape))
    tokens = lambda width, dtype: jax.ShapeDtypeStruct((batch * seq, width), dtype)
    u, w, qd, kd, qk, egl = pl.pallas_call(
        functools.partial(_gdn_prep_kernel, heads=heads, conv_k=conv_k, chunk=chunk, valid_last=valid_last),
        grid=(batch, ns),
        in_specs=[
            tok(3 * gw, raw_c),
            pl.BlockSpec((SUBLANES, 3 * gw),
                         lambda i, c: (jnp.maximum((i * ns + c) * (rows // SUBLANES) - 1, 0), raw_c)),
            pl.BlockSpec((None, SUBLANES, 3 * gw), lambda i, c: (i, 0, 0)),
            tok(LANES, a_c), tok(LANES, b_c),
            const((conv_k, 3 * gw)), const((1, LANES)), const((1, LANES)),
        ],
        out_specs=[tok(gw, 0), tok(gw, 0), tok(gw, 0), tok(gw, 0), tok(heads * chunk, 0),
                   pl.BlockSpec((cps, SUBLANES, LANES), lambda i, c: (i * ns + c, 0, 0))],
        out_shape=[tokens(gw, F32), tokens(gw, BF16), tokens(gw, BF16), tokens(gw, BF16),
                   tokens(heads * chunk, F32),
                   jax.ShapeDtypeStruct((batch * nc, SUBLANES, LANES), F32)],
        scratch_shapes=[pltpu.VMEM((SUBLANES + rows, 3 * gw), F32)],
        compiler_params=_params("parallel", "arbitrary"),
        name="gdn_prep",
    )(raw_a, raw_a, buf, a_a, b_a, conv_w, lane_vec(a_log), lane_vec(dt_bias))

    bt = 2 if batch % 2 == 0 else 1
    nz = z_a.shape[1]
    seq3 = lambda arr: arr.reshape(batch, seq, arr.shape[-1])
    tok3 = lambda width, col=0: pl.BlockSpec((bt, rows, width), lambda i, c: (i, c, col))
    state = pl.BlockSpec((bt, heads, LANES, LANES), lambda i, c: (i, 0, 0, 0))
    o, s_new = pl.pallas_call(
        functools.partial(_gdn_scan_kernel, heads=heads, chunk=chunk),
        grid=(batch // bt, ns),
        in_specs=[tok3(gw), tok3(gw), tok3(gw), tok3(gw), tok3(heads * chunk),
                  pl.BlockSpec((bt, cps, SUBLANES, LANES), lambda i, c: (i, c, 0, 0)),
                  tok3(gw, z_c), state, const((1, LANES))],
        out_specs=[tok3(gw), state],
        out_shape=[jax.ShapeDtypeStruct((batch, seq, gw), BF16), jax.ShapeDtypeStruct(s0.shape, F32)],
        compiler_params=_params("parallel", "arbitrary"),
        name="gdn_scan",
    )(seq3(u), seq3(w), seq3(qd), seq3(kd), seq3(qk), egl.reshape(batch, nc, SUBLANES, LANES),
      z_a.reshape(batch, seq, nz), s0.astype(F32), out_norm.astype(F32).reshape(1, LANES))
    return o.reshape(batch * seq, gw), s_new


def _mem_attn_kernel(q_ref, k_ref, v_ref, o_ref, *, scale):
    for h in range(k_ref.shape[0]):
        sl = slice(h * LANES, (h + 1) * LANES)
        q = (q_ref[:, sl] * scale).astype(BF16)
        z = lax.dot_general(q, k_ref[h].astype(BF16), _NT, preferred_element_type=F32)
        e = jnp.exp(z - jnp.max(z, axis=-1, keepdims=True))
        p = e / jnp.sum(e, axis=-1, keepdims=True)
        o_ref[:, sl] = _dot(p.astype(BF16), v_ref[h].astype(BF16)).astype(o_ref.dtype)


def _mem_attn(q, mem_k, mem_v, *, batch, seq):
    q_a, q_c = q
    _, heads, n_mem, _ = mem_k.shape
    xw = heads * LANES
    tq = _pick(seq, 512)
    nq = seq // tq
    mem_spec = pl.BlockSpec((None, heads, n_mem, LANES), lambda i, j: (i, 0, 0, 0))
    return pl.pallas_call(
        functools.partial(_mem_attn_kernel, scale=1.0 / math.sqrt(LANES)),
        grid=(batch, nq),
        in_specs=[pl.BlockSpec((tq, xw), lambda i, j: (i * nq + j, q_c)), mem_spec, mem_spec],
        out_specs=pl.BlockSpec((tq, xw), lambda i, j: (i * nq + j, 0)),
        out_shape=jax.ShapeDtypeStruct((batch * seq, xw), BF16),
        compiler_params=_params("parallel", "parallel"),
        name="mem_attn",
    )(q_a, mem_k, mem_v)


def _merge_kernel(x_ref, osb_ref, ogdn_ref, ox_ref, gsb_ref, ggdn_ref, gx_ref,
                  wsb_ref, wgdn_ref, wx_ref, wout_ref, o_ref):
    merged = (jax.nn.sigmoid(gsb_ref[...]) * _dot(osb_ref[...], wsb_ref[...])
              + jax.nn.sigmoid(ggdn_ref[...]) * _dot(ogdn_ref[...], wgdn_ref[...])
              + jax.nn.sigmoid(gx_ref[...]) * _dot(ox_ref[...], wx_ref[...]))
    o_ref[...] = x_ref[...] + _dot(merged.astype(BF16), wout_ref[...])


def _merge(x, o_sb, o_gdn, o_x, proj, gate_col, w_up_sb, w_up_gdn, w_up_x, w_out):
    t, d = x.shape
    tm = _pick(t, 256)
    tok = lambda a: pl.BlockSpec((tm, a.shape[1]), lambda i: (i, 0))
    gate = lambda n: pl.BlockSpec((tm, d), lambda i: (i, gate_col + n))
    weight = lambda w: pl.BlockSpec(w.shape, lambda i: (0, 0), pipeline_mode=pl.Buffered(1))
    return pl.pallas_call(
        _merge_kernel,
        grid=(t // tm,),
        in_specs=[tok(x), tok(o_sb), tok(o_gdn), tok(o_x), gate(0), gate(1), gate(2),
                  weight(w_up_sb), weight(w_up_gdn), weight(w_up_x), weight(w_out)],
        out_specs=tok(x),
        out_shape=jax.ShapeDtypeStruct((t, d), F32),
        compiler_params=_params("parallel"),
        name="merge",
    )(x, o_sb, o_gdn, o_x, proj, proj, proj, w_up_sb, w_up_gdn, w_up_x, w_out)


def _split_w_in_kernel(w_ref, sb_ref, rest_ref, *, moves, lane_pads):
    x = w_ref[...]
    sb_ref[...] = x[:, :sb_ref.shape[1]].astype(BF16)
    for src, dst, width in moves:
        rest_ref[:, dst:dst + width] = x[:, src:src + width].astype(BF16)
    lane = lax.broadcasted_iota(jnp.int32, (x.shape[0], LANES), 1)
    for src, dst, n in lane_pads:
        rest_ref[:, dst:dst + LANES] = jnp.where(lane < n, x[:, src:src + LANES], 0.0).astype(BF16)


def _split_w_in(w_in, sbw, gw, xw, d, g_heads):
    rows, n_in = w_in.shape
    src_gqkv = 3 * sbw
    src_a = src_gqkv + 3 * gw
    src_b = src_a + g_heads
    src_z = src_b + g_heads
    src_xq = src_z + gw
    src_gate = src_xq + xw
    assert n_in == src_gate + 3 * d and src_b + LANES <= n_in
    dst_z = 3 * d
    dst_gqkv = dst_z + gw
    dst_xq = dst_gqkv + 3 * gw
    dst_a = dst_xq + xw
    moves = ((src_gate, 0, 3 * d), (src_z, dst_z, gw), (src_gqkv, dst_gqkv, 3 * gw), (src_xq, dst_xq, xw))
    lane_pads = ((src_a, dst_a, g_heads), (src_b, dst_a + LANES, g_heads))
    tr = _pick(rows, 128)
    return pl.pallas_call(
        functools.partial(_split_w_in_kernel, moves=moves, lane_pads=lane_pads),
        grid=(rows // tr,),
        in_specs=[pl.BlockSpec((tr, n_in), lambda i: (i, 0))],
        out_specs=[pl.BlockSpec((tr, 3 * sbw), lambda i: (i, 0)),
                   pl.BlockSpec((tr, dst_a + 2 * LANES), lambda i: (i, 0))],
        out_shape=[jax.ShapeDtypeStruct((rows, 3 * sbw), BF16),
                   jax.ShapeDtypeStruct((rows, dst_a + 2 * LANES), BF16)],
        compiler_params=_params("parallel"),
        name="split_w_in",
    )(w_in)


def kernel(x_prompt, x_sample, cache_sb_k, cache_sb_v, cache_mem_k, cache_mem_v, state_gdn_S, state_gdn_conv, page_table, mem_prompt, ffn1_norm, ffn1_w_in, ffn1_w_out, mix_norm, w_in, sb_logit_bias, gdn_conv_w, gdn_a_log, gdn_dt_bias, gdn_out_norm, mem_norm, w_mem_kv, w_up_sb, w_up_gdn, w_up_x, w_out, ffn2_norm, ffn2_w_in, ffn2_w_out, final_norm):
    batch, seq, d = x_prompt.shape
    dec_batch, dec_seq, _ = x_sample.shape
    depth = w_in.shape[0]
    sb_heads = cache_sb_k.shape[3]
    x_heads = cache_mem_k.shape[3]
    g_heads = state_gdn_S.shape[2]
    n_mem = mem_prompt.shape[1]
    conv_k = gdn_conv_w.shape[1]
    sbw, gw, xw = sb_heads * LANES, g_heads * LANES, x_heads * LANES

    off_gate, off_z = 0, 3 * d
    off_gdn = off_z + gw
    off_xq = off_gdn + 3 * gw
    off_a, off_b = off_xq + xw, off_xq + xw + LANES
    n_proj = off_b + LANES
    assert off_z % gw == 0 and off_gdn % (3 * gw) == 0 and off_xq % xw == 0
    tn_proj = 3 * 256
    assert n_proj % tn_proj == 0

    xp = x_prompt.reshape(batch * seq, d)
    xs = x_sample.reshape(dec_batch * dec_seq, d)
    outs = [[] for _ in range(10)]
    for l in range(depth):
        bf = lambda w: w[l].astype(BF16)
        w1_in, w1_out, w2_in, w2_out = bf(ffn1_w_in), bf(ffn1_w_out), bf(ffn2_w_in), bf(ffn2_w_out)
        w_sb, w_rest = _split_w_in(w_in[l], sbw, gw, xw, d, g_heads)
        wup_sb, wup_gdn, wup_x, wo = bf(w_up_sb), bf(w_up_gdn), bf(w_up_x), bf(w_out)
        last = l == depth - 1
        mixer_w = (gdn_conv_w[l], gdn_a_log[l], gdn_dt_bias[l], gdn_out_norm[l])

        def mix_tail(x, proj, o_sb, o_gdn, o_x):
            x = _merge(x, o_sb, o_gdn, o_x, proj, off_gate // d, wup_sb, wup_gdn, wup_x, wo)
            return _ffn(x, ffn2_norm[l], w2_in, w2_out, final_norm, final=last)

        mk_p, mv_p = _normed_matmul_heads(mem_prompt.reshape(batch * n_mem, d), mem_norm[l], bf(w_mem_kv),
                                          sections=2, batch=batch, seq=n_mem, name="mem_kv")
        xp = _ffn(xp, ffn1_norm[l], w1_in, w1_out, final_norm, final=False)
        q_p, k_p, v_p = _normed_matmul_heads(xp, mix_norm[l], w_sb, sections=3, batch=batch, seq=seq,
                                             name="sb_proj")
        proj = _normed_matmul(xp, mix_norm[l], w_rest, tn=tn_proj, name="mixer_proj")
        o_sb = _sb_prompt(q_p, k_p, v_p, sb_logit_bias[l].astype(F32))
        o_gdn, s_p = _gdn((proj, off_gdn // (3 * gw)), (proj, off_a // LANES), (proj, off_b // LANES),
                          (proj, off_z // gw), jnp.zeros((batch, conv_k - 1, 3 * gw), F32),
                          jnp.zeros((batch, g_heads, LANES, LANES), F32), *mixer_w,
                          batch=batch, seq=seq, valid_last=GDN_CHUNK)
        o_x = _mem_attn((proj, off_xq // xw), mk_p, mv_p, batch=batch, seq=seq)
        xp = mix_tail(xp, proj, o_sb, o_gdn, o_x)
        seq_major = lambda a: a.transpose(0, 2, 1, 3)
        outs[0].append(seq_major(k_p))
        outs[1].append(seq_major(v_p))
        outs[2].append(seq_major(mk_p))
        outs[3].append(seq_major(mv_p))
        outs[4].append(s_p)
        assert seq >= conv_k - 1
        outs[5].append(proj.reshape(batch, seq, n_proj)[:, seq - (conv_k - 1):, off_gdn:off_gdn + 3 * gw])

        xs = _ffn(xs, ffn1_norm[l], w1_in, w1_out, final_norm, final=False)
        qkv = _normed_matmul_heads(xs, mix_norm[l], w_sb, sections=3, batch=1, seq=dec_batch * dec_seq,
                                   name="sb_proj_sample")
        proj = _normed_matmul(xs, mix_norm[l], w_rest, tn=tn_proj, name="mixer_proj_sample")
        q_s, k_s, v_s = (a.reshape(sb_heads, dec_batch, dec_seq, LANES) for a in qkv)
        o_sb = _sb_sample(q_s, k_s, v_s, cache_sb_k[l], cache_sb_v[l], page_table, sb_logit_bias[l])
        o_sb = o_sb.reshape(dec_batch * dec_seq, sbw).astype(BF16)
        proj3 = proj.reshape(dec_batch, dec_seq, n_proj)
        seq_g = _round_up(dec_seq, GDN_CHUNK)
        seq_x = _round_up(dec_seq, SUBLANES)
        padded = lambda lo, hi, n: jnp.pad(proj3[:, :, lo:hi], ((0, 0), (0, n - dec_seq), (0, 0))
                                           ).reshape(dec_batch * n, hi - lo)
        raw_s = proj3[:, :, off_gdn:off_gdn + 3 * gw]
        o_gdn, s_s = _gdn((padded(off_gdn, off_gdn + 3 * gw, seq_g), 0), (padded(off_a, off_a + LANES, seq_g), 0),
                          (padded(off_b, off_b + LANES, seq_g), 0), (padded(off_z, off_z + gw, seq_g), 0),
                          state_gdn_conv[l], state_gdn_S[l], *mixer_w,
                          batch=dec_batch, seq=seq_g, valid_last=dec_seq - (seq_g - GDN_CHUNK))
        o_gdn = o_gdn.reshape(dec_batch, seq_g, gw)[:, :dec_seq].reshape(dec_batch * dec_seq, gw)
        o_x = _mem_attn((padded(off_xq, off_xq + xw, seq_x), 0), cache_mem_k[l].transpose(0, 2, 1, 3),
                        cache_mem_v[l].transpose(0, 2, 1, 3), batch=dec_batch, seq=seq_x)
        o_x = o_x.reshape(dec_batch, seq_x, xw)[:, :dec_seq].reshape(dec_batch * dec_seq, xw)
        xs = mix_tail(xs, proj, o_sb, o_gdn, o_x)
        outs[6].append(k_s.transpose(1, 2, 0, 3))
        outs[7].append(v_s.transpose(1, 2, 0, 3))
        outs[8].append(s_s)
        conv_all = jnp.concatenate([state_gdn_conv[l].astype(F32), raw_s], axis=1)
        outs[9].append(conv_all[:, dec_seq:])

    stack = [jnp.stack(o) for o in outs]
    return (xp.reshape(batch, seq, d), xs.reshape(dec_batch, dec_seq, d), *stack)
```

```python
import functools
import math

import jax
import jax.numpy as jnp
from jax import lax
from jax.experimental import pallas as pl
from jax.experimental.pallas import tpu as pltpu

F32 = jnp.float32
BF16 = jnp.bfloat16
NORM_EPS = 1e-6
LANES = 128
SUBLANES = 8
GDN_CHUNK = 64
VMEM_LIMIT = 56 * 1024 * 1024
HIGHEST = lax.Precision.HIGHEST

_NT = (((1,), (1,)), ((), ()))
_TN = (((0,), (0,)), ((), ()))


def _params(*sem):
    return pltpu.CompilerParams(dimension_semantics=sem, vmem_limit_bytes=VMEM_LIMIT)


def _dot(a, b):
    return jnp.dot(a, b, preferred_element_type=F32)


def _rmsnorm(x, g):
    return x * lax.rsqrt(jnp.mean(x * x, axis=-1, keepdims=True) + NORM_EPS) * g


def _softplus(x):
    return jnp.maximum(x, 0.0) + jnp.log(1.0 + jnp.exp(-jnp.abs(x)))


def _silu(x):
    return x * jax.nn.sigmoid(x)


def _pick(n, pref):
    if n <= pref:
        return n
    t = pref
    while n % t:
        t //= 2
    assert t >= SUBLANES, (n, pref)
    return t


def _divisor(n, pref):
    return max(k for k in range(1, min(n, pref) + 1) if n % k == 0)


def _round_up(n, m):
    return -(-n // m) * m


def _ffn_kernel(x_ref, g_ref, wg_ref, wu_ref, wo_ref, fg_ref, o_ref, xn_ref, *, final):
    f = pl.program_id(1)

    @pl.when(f == 0)
    def _():
        xn_ref[...] = _rmsnorm(x_ref[...], g_ref[...]).astype(BF16)
        o_ref[...] = jnp.zeros_like(o_ref)

    xn = xn_ref[...]
    h = (_silu(_dot(xn, wg_ref[...])) * _dot(xn, wu_ref[...])).astype(BF16)
    o_ref[...] += _dot(h, wo_ref[...])

    @pl.when(f == pl.num_programs(1) - 1)
    def _():
        y = x_ref[...] + 0.5 * o_ref[...]
        if final:
            y = _rmsnorm(y, fg_ref[...])
        o_ref[...] = y


def _ffn(x, g, w_in, w_out, final_g, *, final):
    t, d = x.shape
    dff = w_out.shape[0]
    tm = _pick(t, 512)
    tf = _pick(dff, 512)
    nf = dff // tf
    return pl.pallas_call(
        functools.partial(_ffn_kernel, final=final),
        grid=(t // tm, nf),
        in_specs=[
            pl.BlockSpec((tm, d), lambda i, f: (i, 0)),
            pl.BlockSpec((1, d), lambda i, f: (0, 0)),
            pl.BlockSpec((d, tf), lambda i, f: (0, f)),
            pl.BlockSpec((d, tf), lambda i, f: (0, f + nf)),
            pl.BlockSpec((tf, d), lambda i, f: (f, 0)),
            pl.BlockSpec((1, d), lambda i, f: (0, 0)),
        ],
        out_specs=pl.BlockSpec((tm, d), lambda i, f: (i, 0)),
        out_shape=jax.ShapeDtypeStruct((t, d), F32),
        scratch_shapes=[pltpu.VMEM((tm, d), BF16)],
        compiler_params=_params("parallel", "arbitrary"),
        name="ffn_final" if final else "ffn",
    )(x, g.reshape(1, d), w_in, w_in, w_out, final_g.reshape(1, d))


def _nmm_kernel(x_ref, g_ref, w_ref, *rest):
    *o_refs, xn_ref = rest
    j = pl.program_id(1)

    @pl.when(j == 0)
    def _():
        xn_ref[...] = _rmsnorm(x_ref[...], g_ref[...]).astype(BF16)

    y = _dot(xn_ref[...], w_ref[...])
    if len(o_refs) == 1 and len(o_refs[0].shape) == 2:
        o_refs[0][...] = y
        return
    for s, o_ref in enumerate(o_refs):
        @pl.when(j == s)
        def _():
            for n in range(o_ref.shape[0]):
                o_ref[n] = y[:, n * LANES:(n + 1) * LANES]


def _normed_matmul(x, g, w, *, tn, name):
    t, d = x.shape
    n = w.shape[1]
    assert n % tn == 0 and tn % LANES == 0
    tm = _pick(t, 1024)
    return pl.pallas_call(
        _nmm_kernel,
        grid=(t // tm, n // tn),
        in_specs=[
            pl.BlockSpec((tm, d), lambda i, j: (i, 0)),
            pl.BlockSpec((1, d), lambda i, j: (0, 0)),
            pl.BlockSpec((d, tn), lambda i, j: (0, j)),
        ],
        out_specs=pl.BlockSpec((tm, tn), lambda i, j: (i, j)),
        out_shape=jax.ShapeDtypeStruct((t, n), F32),
        scratch_shapes=[pltpu.VMEM((tm, d), BF16)],
        compiler_params=_params("parallel", "arbitrary"),
        name=name,
    )(x, g.reshape(1, d), w)


def _normed_matmul_heads(x, g, w, *, sections, batch, seq, name):
    t, d = x.shape
    tn = w.shape[1] // sections
    heads = tn // LANES
    assert t == batch * seq and tn % LANES == 0
    tm = _pick(seq, 1024)
    nt = seq // tm
    out_spec = pl.BlockSpec((None, heads, tm, LANES), lambda i, j: (i // nt, 0, i % nt, 0))
    return pl.pallas_call(
        _nmm_kernel,
        grid=(t // tm, sections),
        in_specs=[
            pl.BlockSpec((tm, d), lambda i, j: (i, 0)),
            pl.BlockSpec((1, d), lambda i, j: (0, 0)),
            pl.BlockSpec((d, tn), lambda i, j: (0, j)),
        ],
        out_specs=[out_spec] * sections,
        out_shape=[jax.ShapeDtypeStruct((batch, heads, seq, LANES), F32)] * sections,
        scratch_shapes=[pltpu.VMEM((tm, d), BF16)],
        compiler_params=_params("parallel", "arbitrary"),
        name=name,
    )(x, g.reshape(1, d), w)


LOG2_E = math.log2(math.e)


def _sb_suffix(z, valid, u):
    sp = jnp.maximum(z, 0.0) + jnp.log(1.0 + jnp.exp2(jnp.abs(z) * -LOG2_E))
    if valid is not None:
        sp = jnp.where(valid, sp, 0.0)
    return _dot(sp.astype(BF16), u)


def _sb_exp(z, r, valid):
    w = jnp.exp(z - r)
    if valid is not None:
        w = jnp.where(valid, w, 0.0)
    return w.astype(BF16)


def _suffix_matrix(w_keys, totals=False):
    n = w_keys + (LANES if totals else 0)
    j = lax.broadcasted_iota(jnp.int32, (w_keys, n), 0)
    s = lax.broadcasted_iota(jnp.int32, (w_keys, n), 1)
    return ((j >= s) | (s >= w_keys)).astype(BF16)


def _sb_prompt_kernel(bias_ref, q_ref, k_ref, v_ref, u_ref, o_ref, kbf, vbf, acc, car, *, scale, blk, unroll):
    h = pl.program_id(1)
    i = pl.program_id(2)

    @pl.when(i == 0)
    def _():
        kbf[...] = k_ref[...].astype(BF16)
        vbf[...] = v_ref[...].astype(BF16)

    q = (q_ref[...] * scale).astype(BF16)
    bias = bias_ref[h]
    u = u_ref[...]
    rows = lax.broadcasted_iota(jnp.int32, (blk, blk), 0)
    cols = lax.broadcasted_iota(jnp.int32, (blk, blk), 1)

    def blocks(js, diagonal, carry):
        offs = [pl.multiple_of(j * blk, blk) for j in js]
        valids = [cols < rows if diagonal and n == 0 else None for n in range(len(js))]
        zs = [lax.dot_general(q, kbf[pl.ds(off, blk), :], _NT, preferred_element_type=F32) + bias
              for off in offs]
        sufs = [_sb_suffix(z, valid, u) for z, valid in zip(zs, valids)]
        pv = None
        for off, z, suf, valid in zip(offs, zs, sufs, valids):
            r = suf + jnp.concatenate([carry] * (blk // LANES), axis=1)
            carry = jnp.broadcast_to(r[:, 0:1], carry.shape)
            term = _dot(_sb_exp(z, r, valid), vbf[pl.ds(off, blk), :])
            pv = term if pv is None else pv + term
        return pv, carry

    for r in range(2):
        @pl.when(i % 2 == r)
        def _():
            pv, c = blocks([i - m for m in range(r + 1)], True, jnp.zeros((blk, LANES), F32))
            acc[...] = pv
            car[...] = c

    def group(j, size):
        pv, c = blocks([j - m for m in range(size)], False, car[...])
        acc[...] += pv
        car[...] = c

    j = i - 1 - i % 2
    size = 2
    while size < unroll:
        bit = (i // size) % 2

        @pl.when(bit == 1)
        def _():
            group(j, size)

        j = j - size * bit
        size *= 2

    def body(n, _):
        group(j - unroll * n, unroll)
        return 0

    lax.fori_loop(0, i // unroll, body, 0)
    o_ref[...] = acc[...].astype(o_ref.dtype)


def _sb_prompt(q, k, v, bias):
    batch, heads, seq, _ = q.shape
    blk = _pick(seq, 256)
    assert blk % LANES == 0 and seq % blk == 0
    nq = seq // blk
    whole = pl.BlockSpec((None, None, seq, LANES), lambda b, h, i: (b, h, 0, 0))
    return pl.pallas_call(
        functools.partial(_sb_prompt_kernel, scale=1.0 / math.sqrt(LANES), blk=blk, unroll=16),
        grid=(batch, heads, nq),
        in_specs=[
            pl.BlockSpec(memory_space=pltpu.SMEM),
            pl.BlockSpec((None, None, blk, LANES), lambda b, h, i: (b, h, i, 0)),
            whole, whole,
            pl.BlockSpec((blk, blk), lambda b, h, i: (0, 0)),
        ],
        out_specs=pl.BlockSpec((blk, LANES), lambda b, h, i: (b * nq + i, h)),
        out_shape=jax.ShapeDtypeStruct((batch * seq, heads * LANES), BF16),
        scratch_shapes=[
            pltpu.VMEM((seq, LANES), BF16),
            pltpu.VMEM((seq, LANES), BF16),
            pltpu.VMEM((blk, LANES), F32),
            pltpu.VMEM((blk, LANES), F32),
        ],
        compiler_params=_params("parallel", "parallel", "arbitrary"),
        name="sb_prompt",
    )(bias, q, k, v, _suffix_matrix(blk))


def _sb_sample_kernel(pt_ref, q_ref, bias_ref, tpos_ref, u_ref, kn_ref, vn_ref, *rest, pages):
    k_refs = rest[:pages]
    v_refs = rest[pages:2 * pages]
    o_ref, acc, car = rest[2 * pages:]
    s = pl.program_id(1)
    heads, rph, _ = q_ref.shape
    bias = bias_ref[...]
    u = u_ref[...]

    def logits(k_ref):
        return jnp.concatenate(
            [lax.dot_general(q_ref[h], k_ref[h].astype(BF16), _NT, preferred_element_type=F32)
             for h in range(heads)], axis=0) + bias

    def weighted(z, rc, carry, v_ref, valid):
        w = _sb_exp(z, rc[:, :LANES] + carry, valid)
        return jnp.concatenate(
            [_dot(w[h * rph:(h + 1) * rph], v_ref[h].astype(BF16)) for h in range(heads)], axis=0)

    @pl.when(s == 0)
    def _():
        valid = lax.broadcasted_iota(jnp.int32, tpos_ref.shape, 1) < tpos_ref[...]
        z = logits(kn_ref)
        rc = _sb_suffix(z, valid, u)
        acc[...] = weighted(z, rc, jnp.zeros_like(z), vn_ref, valid)
        car[...] = rc[:, LANES:]

    zs = [logits(k_refs[p]) for p in range(pages)]
    rows = zs[0].shape[0]
    rc_all = _sb_suffix(jnp.concatenate(zs, axis=0), None, u)
    rcs = [rc_all[p * rows:(p + 1) * rows] for p in range(pages)]
    carry = car[...]
    carries = []
    for rc in rcs:
        carries.append(carry)
        carry = carry + rc[:, LANES:]
    car[...] = carry
    pvs = [weighted(z, rc, c, v_refs[p], None) for p, (z, rc, c) in enumerate(zip(zs, rcs, carries))]
    acc[...] += functools.reduce(lambda a, b: a + b, pvs)

    @pl.when(s == pl.num_programs(1) - 1)
    def _():
        o_ref[...] = acc[...]


def _sb_sample(q, k_new, v_new, cache_k, cache_v, page_table, bias):
    h, b, t, _ = q.shape
    n_phys, page = cache_k.shape[:2]
    n_pages = page_table.shape[1]
    assert page == LANES and t <= page
    rph = _round_up(t, SUBLANES)
    rows = h * rph
    pages = _divisor(n_pages, 16)
    scale = 1.0 / math.sqrt(LANES)
    per_head = lambda a, n: jnp.pad(a.transpose(1, 0, 2, 3), ((0, 0), (0, 0), (0, n - t), (0, 0)))
    q_rows = per_head(q * scale, rph).astype(BF16)
    bias_rows = jnp.broadcast_to(jnp.repeat(bias.astype(F32), rph)[:, None], (rows, LANES))
    tpos = jnp.broadcast_to(jnp.tile(jnp.arange(rph, dtype=jnp.int32), h)[:, None], (rows, LANES))
    tpos = jnp.minimum(tpos, t)
    ck = cache_k.transpose(0, 2, 1, 3)
    cv = cache_v.transpose(0, 2, 1, 3)

    def page_spec(p):
        return pl.BlockSpec((None, h, page, LANES),
                            lambda i, s, pt: (pt[i, n_pages - 1 - (s * pages + p)], 0, 0, 0))

    const = lambda shape: pl.BlockSpec(shape, lambda i, s, pt: (0,) * len(shape))
    per_b = lambda *shape: pl.BlockSpec((None,) + shape, lambda i, s, pt: (i,) + (0,) * len(shape))
    o_rows = pl.pallas_call(
        functools.partial(_sb_sample_kernel, pages=pages),
        grid_spec=pltpu.PrefetchScalarGridSpec(
            num_scalar_prefetch=1,
            grid=(b, n_pages // pages),
            in_specs=[per_b(h, rph, LANES), const((rows, LANES)), const((rows, LANES)),
                      const((page, page + LANES)), per_b(h, page, LANES), per_b(h, page, LANES)]
                     + [page_spec(p) for p in range(pages)] * 2,
            out_specs=per_b(rows, LANES),
            scratch_shapes=[pltpu.VMEM((rows, LANES), F32), pltpu.VMEM((rows, LANES), F32)],
        ),
        out_shape=jax.ShapeDtypeStruct((b, rows, LANES), F32),
        compiler_params=_params("parallel", "arbitrary"),
        name="sb_sample",
    )(page_table, q_rows, bias_rows, tpos, _suffix_matrix(page, totals=True),
      per_head(k_new, page), per_head(v_new, page),
      *([ck] * pages), *([cv] * pages))
    return o_rows.reshape(b, h, rph, LANES)[:, :, :t].transpose(0, 2, 1, 3)


def _dot_hi(a, b):
    return jnp.dot(a, b, precision=HIGHEST, preferred_element_type=F32)


def _split(a):
    hi = a.astype(BF16)
    return hi, (a - hi.astype(F32)).astype(BF16)


def _dot3(a, b):
    return _dot(a[0], b[0]) + _dot(a[0], b[1]) + _dot(a[1], b[0])


def _gdn_prep_kernel(raw_ref, prev_ref, buf_ref, a_ref, b_ref, cw_ref, alog_ref, dtb_ref,
                     u_ref, w_ref, qd_ref, kd_ref, qk_ref, egl_ref, xfull,
                     *, heads, conv_k, chunk, valid_last):
    c = pl.program_id(1)
    rows = raw_ref.shape[0]
    n_chunks = rows // chunk
    gw = heads * LANES

    @pl.when(c == 0)
    def _():
        xfull[0:SUBLANES, :] = buf_ref[...]

    @pl.when(c > 0)
    def _():
        xfull[0:SUBLANES, :] = prev_ref[...]

    xfull[SUBLANES:SUBLANES + rows, :] = raw_ref[...]
    x = xfull[...]
    y = x[SUBLANES:] * cw_ref[conv_k - 1:conv_k, :]
    for j in range(conv_k - 1):
        back = pltpu.roll(x, conv_k - 1 - j, axis=0)
        y = y + back[SUBLANES:] * cw_ref[j:j + 1, :]
    y = _silu(y)

    ones = jnp.ones((LANES, LANES), BF16)

    def l2norm(a):
        ss = _dot((a * a).astype(BF16), ones)
        return a * lax.rsqrt(ss + NORM_EPS)

    qn = [l2norm(y[:, h * LANES:(h + 1) * LANES]) * (LANES ** -0.5) for h in range(heads)]
    kn = [l2norm(y[:, gw + h * LANES:gw + (h + 1) * LANES]) for h in range(heads)]

    g = -jnp.exp(alog_ref[...]) * _softplus(a_ref[...] + dtb_ref[...])
    beta = jax.nn.sigmoid(b_ref[...])
    live = None
    if valid_last < chunk:
        last = c == pl.num_programs(1) - 1
        row = lax.broadcasted_iota(jnp.int32, (rows, 1), 0)
        live_rows = row < jnp.where(last, rows - chunk + valid_last, rows)
        g = jnp.where(live_rows, g, 0.0)
        beta = jnp.where(live_rows, beta, 0.0)
        live = lax.broadcasted_iota(jnp.int32, (chunk, 1), 0) < jnp.where(last, valid_last, chunk)

    shift = int(math.log2(chunk))
    ri = lax.broadcasted_iota(jnp.int32, (rows, rows), 0)
    ci = lax.broadcasted_iota(jnp.int32, (rows, rows), 1)
    same = (ri >> shift) == (ci >> shift)
    gc = _dot_hi(jnp.where(same & (ri >= ci), 1.0, 0.0), g)
    sel = (lax.broadcasted_iota(jnp.int32, (SUBLANES, LANES), 0)
           == lax.broadcasted_iota(jnp.int32, (SUBLANES, LANES), 1)).astype(F32)
    gc_rows = lax.dot_general(sel, gc, _NT, precision=HIGHEST, preferred_element_type=F32)

    ri = lax.broadcasted_iota(jnp.int32, (chunk, chunk), 0)
    ci = lax.broadcasted_iota(jnp.int32, (chunk, chunk), 1)
    incl = ri >= ci
    strict = ri > ci

    for n in range(n_chunks):
        egl_ref[n] = jnp.broadcast_to(jnp.exp(gc[(n + 1) * chunk - 1:(n + 1) * chunk, :]), (SUBLANES, LANES))

    chains = [(n, h) for n in range(n_chunks) for h in range(heads)]
    lows, rhss = [], []
    for n, h in chains:
        rs = slice(n * chunk, (n + 1) * chunk)
        sl = slice(h * LANES, (h + 1) * LANES)
        qh = qn[h][rs]
        kh = kn[h][rs]
        vh = y[rs, 2 * gw + h * LANES:2 * gw + (h + 1) * LANES]
        if live is not None and n == n_chunks - 1:
            qh = jnp.where(live, qh, 0.0)
            kh = jnp.where(live, kh, 0.0)
            vh = jnp.where(live, vh, 0.0)
        g_col = gc[rs, h:h + 1]
        g_row = gc_rows[h:h + 1, rs]
        g_last = gc[(n + 1) * chunk - 1:(n + 1) * chunk, h:h + 1]
        decay = jnp.exp(jnp.where(incl, g_col - g_row, 0.0))
        b_col = beta[rs, h:h + 1]
        eg = jnp.exp(g_col)
        kb = kh * b_col
        k_bf = kh.astype(BF16)
        low = lax.dot_general(kb.astype(BF16), k_bf, _NT, preferred_element_type=F32)
        lows.append(low * jnp.where(strict, decay, 0.0))
        rhss.append(_split(jnp.concatenate([vh * b_col, kb * eg], axis=1)))
        qk = lax.dot_general(qh.astype(BF16), k_bf, _NT, preferred_element_type=F32)
        qk_ref[rs, h * chunk:(h + 1) * chunk] = qk * jnp.where(incl, decay, 0.0)
        qd_ref[rs, sl] = (qh * eg).astype(BF16)
        kd_ref[rs, sl] = (kh * jnp.exp(g_last - g_col)).astype(BF16)

    eye_c = jnp.where(ri == ci, 1.0, 0.0)
    pws = [_split(low) for low in lows]
    invs = [eye_c - low for low in lows]
    for _ in range(shift - 1):
        pws = [_split(_dot3(pw, pw)) for pw in pws]
        invs = [inv + _dot3(_split(inv), pw) for inv, pw in zip(invs, pws)]
    for (n, h), inv, rhs in zip(chains, invs, rhss):
        rs = slice(n * chunk, (n + 1) * chunk)
        sl = slice(h * LANES, (h + 1) * LANES)
        x = _dot3(_split(inv), rhs)
        u_ref[rs, sl] = x[:, :LANES]
        w_ref[rs, sl] = x[:, LANES:].astype(BF16)


def _gdn_scan_kernel(u_ref, w_ref, qd_ref, kd_ref, qk_ref, egl_ref, z_ref, s0_ref, on_ref,
                     o_ref, s_ref, *, heads, chunk):
    bt, rows, _ = u_ref.shape

    @pl.when(pl.program_id(1) == 0)
    def _():
        s_ref[...] = s0_ref[...]

    chains = [(bi, h, slice(h * LANES, (h + 1) * LANES)) for bi in range(bt) for h in range(heads)]
    for n in range(rows // chunk):
        rs = slice(n * chunk, (n + 1) * chunk)
        s_old = [s_ref[bi, h] for bi, h, _ in chains]
        s_bf = [s.astype(BF16) for s in s_old]
        ws = [_dot(w_ref[bi, rs, sl], s) for (bi, _, sl), s in zip(chains, s_bf)]
        qs = [_dot(qd_ref[bi, rs, sl], s) for (bi, _, sl), s in zip(chains, s_bf)]
        v_new = [(u_ref[bi, rs, sl] - x).astype(BF16) for (bi, _, sl), x in zip(chains, ws)]
        os = [x + _dot(qk_ref[bi, rs, h * chunk:(h + 1) * chunk].astype(BF16), v)
              for (bi, h, _), x, v in zip(chains, qs, v_new)]
        upd = [lax.dot_general(kd_ref[bi, rs, sl], v, _TN, preferred_element_type=F32)
               for (bi, _, sl), v in zip(chains, v_new)]
        for (bi, h, sl), s, d, o_h in zip(chains, s_old, upd, os):
            s_ref[bi, h] = s * egl_ref[bi, n, 0:1, h:h + 1] + d
            o_h = _rmsnorm(o_h, on_ref[...]) * _silu(z_ref[bi, rs, sl])
            o_ref[bi, rs, sl] = o_h.astype(o_ref.dtype)


def _gdn(raw, a, b, z, conv_buf, s0, conv_w, a_log, dt_bias, out_norm, *, batch, seq, valid_last):
    heads = s0.shape[1]
    gw = heads * LANES
    conv_k = conv_w.shape[0]
    chunk = GDN_CHUNK
    assert seq % chunk == 0 and heads <= SUBLANES and conv_k - 1 <= SUBLANES
    nc = seq // chunk
    (raw_a, raw_c), (a_a, a_c), (b_a, b_c), (z_a, z_c) = raw, a, b, z
    lane_vec = lambda v: jnp.pad(v.astype(F32), (0, LANES - heads)).reshape(1, LANES)
    buf = jnp.pad(conv_buf, ((0, 0), (SUBLANES - (conv_k - 1), 0), (0, 0)))

    cps = _divisor(nc, 4)
    rows = cps * chunk
    ns = nc // cps
    tok = lambda width, col: pl.BlockSpec((rows, width), lambda i, c: (i * ns + c, col))
    const = lambda shape: pl.BlockSpec(shape, lambda i, c: (0,) * len(shape))
    tokens = lambda width, dtype: jax.ShapeDtypeStruct((batch * seq, width), dtype)
    u, w, qd, kd, qk, egl = pl.pallas_call(
        functools.partial(_gdn_prep_kernel, heads=heads, conv_k=conv_k, chunk=chunk, valid_last=valid_last),
        grid=(batch, ns),
        in_specs=[
            tok(3 * gw, raw_c),
            pl.BlockSpec((SUBLANES, 3 * gw),
                         lambda i, c: (jnp.maximum((i * ns + c) * (rows // SUBLANES) - 1, 0), raw_c)),
            pl.BlockSpec((None, SUBLANES, 3 * gw), lambda i, c: (i, 0, 0)),
            tok(LANES, a_c), tok(LANES, b_c),
            const((conv_k, 3 * gw)), const((1, LANES)), const((1, LANES)),
        ],
        out_specs=[tok(gw, 0), tok(gw, 0), tok(gw, 0), tok(gw, 0), tok(heads * chunk, 0),
                   pl.BlockSpec((cps, SUBLANES, LANES), lambda i, c: (i * ns + c, 0, 0))],
        out_shape=[tokens(gw, F32), tokens(gw, BF16), tokens(gw, BF16), tokens(gw, BF16),
                   tokens(heads * chunk, F32),
                   jax.ShapeDtypeStruct((batch * nc, SUBLANES, LANES), F32)],
        scratch_shapes=[pltpu.VMEM((SUBLANES + rows, 3 * gw), F32)],
        compiler_params=_params("parallel", "arbitrary"),
        name="gdn_prep",
    )(raw_a, raw_a, buf, a_a, b_a, conv_w, lane_vec(a_log), lane_vec(dt_bias))

    bt = 2 if batch % 2 == 0 else 1
    nz = z_a.shape[1]
    seq3 = lambda arr: arr.reshape(batch, seq, arr.shape[-1])
    tok3 = lambda width, col=0: pl.BlockSpec((bt, rows, width), lambda i, c: (i, c, col))
    state = pl.BlockSpec((bt, heads, LANES, LANES), lambda i, c: (i, 0, 0, 0))
    o, s_new = pl.pallas_call(
        functools.partial(_gdn_scan_kernel, heads=heads, chunk=chunk),
        grid=(batch // bt, ns),
        in_specs=[tok3(gw), tok3(gw), tok3(gw), tok3(gw), tok3(heads * chunk),
                  pl.BlockSpec((bt, cps, SUBLANES, LANES), lambda i, c: (i, c, 0, 0)),
                  tok3(gw, z_c), state, const((1, LANES))],
        out_specs=[tok3(gw), state],
        out_shape=[jax.ShapeDtypeStruct((batch, seq, gw), BF16), jax.ShapeDtypeStruct(s0.shape, F32)],
        compiler_params=_params("parallel", "arbitrary"),
        name="gdn_scan",
    )(seq3(u), seq3(w), seq3(qd), seq3(kd), seq3(qk), egl.reshape(batch, nc, SUBLANES, LANES),
      z_a.reshape(batch, seq, nz), s0.astype(F32), out_norm.astype(F32).reshape(1, LANES))
    return o.reshape(batch * seq, gw), s_new


def _mem_attn_kernel(q_ref, k_ref, v_ref, o_ref, *, scale):
    for h in range(k_ref.shape[0]):
        sl = slice(h * LANES, (h + 1) * LANES)
        q = (q_ref[:, sl] * scale).astype(BF16)
        z = lax.dot_general(q, k_ref[h].astype(BF16), _NT, preferred_element_type=F32)
        e = jnp.exp(z - jnp.max(z, axis=-1, keepdims=True))
        p = e / jnp.sum(e, axis=-1, keepdims=True)
        o_ref[:, sl] = _dot(p.astype(BF16), v_ref[h].astype(BF16)).astype(o_ref.dtype)


def _mem_attn(q, mem_k, mem_v, *, batch, seq):
    q_a, q_c = q
    _, heads, n_mem, _ = mem_k.shape
    xw = heads * LANES
    tq = _pick(seq, 512)
    nq = seq // tq
    mem_spec = pl.BlockSpec((None, heads, n_mem, LANES), lambda i, j: (i, 0, 0, 0))
    return pl.pallas_call(
        functools.partial(_mem_attn_kernel, scale=1.0 / math.sqrt(LANES)),
        grid=(batch, nq),
        in_specs=[pl.BlockSpec((tq, xw), lambda i, j: (i * nq + j, q_c)), mem_spec, mem_spec],
        out_specs=pl.BlockSpec((tq, xw), lambda i, j: (i * nq + j, 0)),
        out_shape=jax.ShapeDtypeStruct((batch * seq, xw), BF16),
        compiler_params=_params("parallel", "parallel"),
        name="mem_attn",
    )(q_a, mem_k, mem_v)


def _merge_kernel(x_ref, osb_ref, ogdn_ref, ox_ref, gsb_ref, ggdn_ref, gx_ref,
                  wsb_ref, wgdn_ref, wx_ref, wout_ref, o_ref):
    merged = (jax.nn.sigmoid(gsb_ref[...]) * _dot(osb_ref[...], wsb_ref[...])
              + jax.nn.sigmoid(ggdn_ref[...]) * _dot(ogdn_ref[...], wgdn_ref[...])
              + jax.nn.sigmoid(gx_ref[...]) * _dot(ox_ref[...], wx_ref[...]))
    o_ref[...] = x_ref[...] + _dot(merged.astype(BF16), wout_ref[...])


def _merge(x, o_sb, o_gdn, o_x, proj, gate_col, w_up_sb, w_up_gdn, w_up_x, w_out):
    t, d = x.shape
    tm = _pick(t, 256)
    tok = lambda a: pl.BlockSpec((tm, a.shape[1]), lambda i: (i, 0))
    gate = lambda n: pl.BlockSpec((tm, d), lambda i: (i, gate_col + n))
    weight = lambda w: pl.BlockSpec(w.shape, lambda i: (0, 0), pipeline_mode=pl.Buffered(1))
    return pl.pallas_call(
        _merge_kernel,
        grid=(t // tm,),
        in_specs=[tok(x), tok(o_sb), tok(o_gdn), tok(o_x), gate(0), gate(1), gate(2),
                  weight(w_up_sb), weight(w_up_gdn), weight(w_up_x), weight(w_out)],
        out_specs=tok(x),
        out_shape=jax.ShapeDtypeStruct((t, d), F32),
        compiler_params=_params("parallel"),
        name="merge",
    )(x, o_sb, o_gdn, o_x, proj, proj, proj, w_up_sb, w_up_gdn, w_up_x, w_out)


def _split_w_in_kernel(w_ref, sb_ref, rest_ref, *, moves, lane_pads):
    x = w_ref[...]
    sb_ref[...] = x[:, :sb_ref.shape[1]].astype(BF16)
    for src, dst, width in moves:
        rest_ref[:, dst:dst + width] = x[:, src:src + width].astype(BF16)
    lane = lax.broadcasted_iota(jnp.int32, (x.shape[0], LANES), 1)
    for src, dst, n in lane_pads:
        rest_ref[:, dst:dst + LANES] = jnp.where(lane < n, x[:, src:src + LANES], 0.0).astype(BF16)


def _split_w_in(w_in, sbw, gw, xw, d, g_heads):
    rows, n_in = w_in.shape
    src_gqkv = 3 * sbw
    src_a = src_gqkv + 3 * gw
    src_b = src_a + g_heads
    src_z = src_b + g_heads
    src_xq = src_z + gw
    src_gate = src_xq + xw
    assert n_in == src_gate + 3 * d and src_b + LANES <= n_in
    dst_z = 3 * d
    dst_gqkv = dst_z + gw
    dst_xq = dst_gqkv + 3 * gw
    dst_a = dst_xq + xw
    moves = ((src_gate, 0, 3 * d), (src_z, dst_z, gw), (src_gqkv, dst_gqkv, 3 * gw), (src_xq, dst_xq, xw))
    lane_pads = ((src_a, dst_a, g_heads), (src_b, dst_a + LANES, g_heads))
    tr = _pick(rows, 128)
    return pl.pallas_call(
        functools.partial(_split_w_in_kernel, moves=moves, lane_pads=lane_pads),
        grid=(rows // tr,),
        in_specs=[pl.BlockSpec((tr, n_in), lambda i: (i, 0))],
        out_specs=[pl.BlockSpec((tr, 3 * sbw), lambda i: (i, 0)),
                   pl.BlockSpec((tr, dst_a + 2 * LANES), lambda i: (i, 0))],
        out_shape=[jax.ShapeDtypeStruct((rows, 3 * sbw), BF16),
                   jax.ShapeDtypeStruct((rows, dst_a + 2 * LANES), BF16)],
        compiler_params=_params("parallel"),
        name="split_w_in",
    )(w_in)


def kernel(x_prompt, x_sample, cache_sb_k, cache_sb_v, cache_mem_k, cache_mem_v, state_gdn_S, state_gdn_conv, page_table, mem_prompt, ffn1_norm, ffn1_w_in, ffn1_w_out, mix_norm, w_in, sb_logit_bias, gdn_conv_w, gdn_a_log, gdn_dt_bias, gdn_out_norm, mem_norm, w_mem_kv, w_up_sb, w_up_gdn, w_up_x, w_out, ffn2_norm, ffn2_w_in, ffn2_w_out, final_norm):
    batch, seq, d = x_prompt.shape
    dec_batch, dec_seq, _ = x_sample.shape
    depth = w_in.shape[0]
    sb_heads = cache_sb_k.shape[3]
    x_heads = cache_mem_k.shape[3]
    g_heads = state_gdn_S.shape[2]
    n_mem = mem_prompt.shape[1]
    conv_k = gdn_conv_w.shape[1]
    sbw, gw, xw = sb_heads * LANES, g_heads * LANES, x_heads * LANES

    off_gate, off_z = 0, 3 * d
    off_gdn = off_z + gw
    off_xq = off_gdn + 3 * gw
    off_a, off_b = off_xq + xw, off_xq + xw + LANES
    n_proj = off_b + LANES
    assert off_z % gw == 0 and off_gdn % (3 * gw) == 0 and off_xq % xw == 0
    tn_proj = 3 * 256
    assert n_proj % tn_proj == 0

    xp = x_prompt.reshape(batch * seq, d)
    xs = x_sample.reshape(dec_batch * dec_seq, d)
    outs = [[] for _ in range(10)]
    for l in range(depth):
        bf = lambda w: w[l].astype(BF16)
        w1_in, w1_out, w2_in, w2_out = bf(ffn1_w_in), bf(ffn1_w_out), bf(ffn2_w_in), bf(ffn2_w_out)
        w_sb, w_rest = _split_w_in(w_in[l], sbw, gw, xw, d, g_heads)
        wup_sb, wup_gdn, wup_x, wo = bf(w_up_sb), bf(w_up_gdn), bf(w_up_x), bf(w_out)
        last = l == depth - 1
        mixer_w = (gdn_conv_w[l], gdn_a_log[l], gdn_dt_bias[l], gdn_out_norm[l])

        def mix_tail(x, proj, o_sb, o_gdn, o_x):
            x = _merge(x, o_sb, o_gdn, o_x, proj, off_gate // d, wup_sb, wup_gdn, wup_x, wo)
            return _ffn(x, ffn2_norm[l], w2_in, w2_out, final_norm, final=last)

        mk_p, mv_p = _normed_matmul_heads(mem_prompt.reshape(batch * n_mem, d), mem_norm[l], bf(w_mem_kv),
                                          sections=2, batch=batch, seq=n_mem, name="mem_kv")
        xp = _ffn(xp, ffn1_norm[l], w1_in, w1_out, final_norm, final=False)
        q_p, k_p, v_p = _normed_matmul_heads(xp, mix_norm[l], w_sb, sections=3, batch=batch, seq=seq,
                                             name="sb_proj")
        proj = _normed_matmul(xp, mix_norm[l], w_rest, tn=tn_proj, name="mixer_proj")
        o_sb = _sb_prompt(q_p, k_p, v_p, sb_logit_bias[l].astype(F32))
        o_gdn, s_p = _gdn((proj, off_gdn // (3 * gw)), (proj, off_a // LANES), (proj, off_b // LANES),
                          (proj, off_z // gw), jnp.zeros((batch, conv_k - 1, 3 * gw), F32),
                          jnp.zeros((batch, g_heads, LANES, LANES), F32), *mixer_w,
                          batch=batch, seq=seq, valid_last=GDN_CHUNK)
        o_x = _mem_attn((proj, off_xq // xw), mk_p, mv_p, batch=batch, seq=seq)
        xp = mix_tail(xp, proj, o_sb, o_gdn, o_x)
        seq_major = lambda a: a.transpose(0, 2, 1, 3)
        outs[0].append(seq_major(k_p))
        outs[1].append(seq_major(v_p))
        outs[2].append(seq_major(mk_p))
        outs[3].append(seq_major(mv_p))
        outs[4].append(s_p)
        assert seq >= conv_k - 1
        outs[5].append(proj.reshape(batch, seq, n_proj)[:, seq - (conv_k - 1):, off_gdn:off_gdn + 3 * gw])

        xs = _ffn(xs, ffn1_norm[l], w1_in, w1_out, final_norm, final=False)
        qkv = _normed_matmul_heads(xs, mix_norm[l], w_sb, sections=3, batch=1, seq=dec_batch * dec_seq,
                                   name="sb_proj_sample")
        proj = _normed_matmul(xs, mix_norm[l], w_rest, tn=tn_proj, name="mixer_proj_sample")
        q_s, k_s, v_s = (a.reshape(sb_heads, dec_batch, dec_seq, LANES) for a in qkv)
        o_sb = _sb_sample(q_s, k_s, v_s, cache_sb_k[l], cache_sb_v[l], page_table, sb_logit_bias[l])
        o_sb = o_sb.reshape(dec_batch * dec_seq, sbw).astype(BF16)
        proj3 = proj.reshape(dec_batch, dec_seq, n_proj)
        seq_g = _round_up(dec_seq, GDN_CHUNK)
        seq_x = _round_up(dec_seq, SUBLANES)
        padded = lambda lo, hi, n: jnp.pad(proj3[:, :, lo:hi], ((0, 0), (0, n - dec_seq), (0, 0))
                                           ).reshape(dec_batch * n, hi - lo)
        raw_s = proj3[:, :, off_gdn:off_gdn + 3 * gw]
        o_gdn, s_s = _gdn((padded(off_gdn, off_gdn + 3 * gw, seq_g), 0), (padded(off_a, off_a + LANES, seq_g), 0),
                          (padded(off_b, off_b + LANES, seq_g), 0), (padded(off_z, off_z + gw, seq_g), 0),
                          state_gdn_conv[l], state_gdn_S[l], *mixer_w,
                          batch=dec_batch, seq=seq_g, valid_last=dec_seq - (seq_g - GDN_CHUNK))
        o_gdn = o_gdn.reshape(dec_batch, seq_g, gw)[:, :dec_seq].reshape(dec_batch * dec_seq, gw)
        o_x = _mem_attn((padded(off_xq, off_xq + xw, seq_x), 0), cache_mem_k[l].transpose(0, 2, 1, 3),
                        cache_mem_v[l].transpose(0, 2, 1, 3), batch=dec_batch, seq=seq_x)
        o_x = o_x.reshape(dec_batch, seq_x, xw)[:, :dec_seq].reshape(dec_batch * dec_seq, xw)
        xs = mix_tail(xs, proj, o_sb, o_gdn, o_x)
        outs[6].append(k_s.transpose(1, 2, 0, 3))
        outs[7].append(v_s.transpose(1, 2, 0, 3))
        outs[8].append(s_s)
        conv_all = jnp.concatenate([state_gdn_conv[l].astype(F32), raw_s], axis=1)
        outs[9].append(conv_all[:, dec_seq:])

    stack = [jnp.stack(o) for o in outs]
    return (xp.reshape(batch, seq, d), xs.reshape(dec_batch, dec_seq, d), *stack)
```

```python
import functools
import math

import jax
import jax.numpy as jnp
from jax import lax
from jax.experimental import pallas as pl
from jax.experimental.pallas import tpu as pltpu

F32 = jnp.float32
BF16 = jnp.bfloat16
NORM_EPS = 1e-6
LANES = 128
SUBLANES = 8
GDN_CHUNK = 64
VMEM_LIMIT = 56 * 1024 * 1024
HIGHEST = lax.Precision.HIGHEST

_NT = (((1,), (1,)), ((), ()))
_TN = (((0,), (0,)), ((), ()))


def _params(*sem):
    return pltpu.CompilerParams(dimension_semantics=sem, vmem_limit_bytes=VMEM_LIMIT)


def _dot(a, b):
    return jnp.dot(a, b, preferred_element_type=F32)


def _rmsnorm(x, g):
    return x * lax.rsqrt(jnp.mean(x * x, axis=-1, keepdims=True) + NORM_EPS) * g


def _softplus(x):
    return jnp.maximum(x, 0.0) + jnp.log(1.0 + jnp.exp(-jnp.abs(x)))


def _silu(x):
    return x * jax.nn.sigmoid(x)


def _pick(n, pref):
    if n <= pref:
        return n
    t = pref
    while n % t:
        t //= 2
    assert t >= SUBLANES, (n, pref)
    return t


def _divisor(n, pref):
    return max(k for k in range(1, min(n, pref) + 1) if n % k == 0)


def _round_up(n, m):
    return -(-n // m) * m


def _ffn_kernel(x_ref, g_ref, wg_ref, wu_ref, wo_ref, fg_ref, o_ref, xn_ref, *, final):
    f = pl.program_id(1)

    @pl.when(f == 0)
    def _():
        xn_ref[...] = _rmsnorm(x_ref[...], g_ref[...]).astype(BF16)
        o_ref[...] = jnp.zeros_like(o_ref)

    xn = xn_ref[...]
    h = (_silu(_dot(xn, wg_ref[...])) * _dot(xn, wu_ref[...])).astype(BF16)
    o_ref[...] += _dot(h, wo_ref[...])

    @pl.when(f == pl.num_programs(1) - 1)
    def _():
        y = x_ref[...] + 0.5 * o_ref[...]
        if final:
            y = _rmsnorm(y, fg_ref[...])
        o_ref[...] = y


def _ffn(x, g, w_in, w_out, final_g, *, final):
    t, d = x.shape
    dff = w_out.shape[0]
    tm = _pick(t, 512)
    tf = _pick(dff, 512)
    nf = dff // tf
    return pl.pallas_call(
        functools.partial(_ffn_kernel, final=final),
        grid=(t // tm, nf),
        in_specs=[
            pl.BlockSpec((tm, d), lambda i, f: (i, 0)),
            pl.BlockSpec((1, d), lambda i, f: (0, 0)),
            pl.BlockSpec((d, tf), lambda i, f: (0, f)),
            pl.BlockSpec((d, tf), lambda i, f: (0, f + nf)),
            pl.BlockSpec((tf, d), lambda i, f: (f, 0)),
            pl.BlockSpec((1, d), lambda i, f: (0, 0)),
        ],
        out_specs=pl.BlockSpec((tm, d), lambda i, f: (i, 0)),
        out_shape=jax.ShapeDtypeStruct((t, d), F32),
        scratch_shapes=[pltpu.VMEM((tm, d), BF16)],
        compiler_params=_params("parallel", "arbitrary"),
        name="ffn_final" if final else "ffn",
    )(x, g.reshape(1, d), w_in, w_in, w_out, final_g.reshape(1, d))


def _nmm_kernel(x_ref, g_ref, w_ref, *rest):
    *o_refs, xn_ref = rest
    j = pl.program_id(1)

    @pl.when(j == 0)
    def _():
        xn_ref[...] = _rmsnorm(x_ref[...], g_ref[...]).astype(BF16)

    y = lax.dot_general(xn_ref[...], w_ref[...], _NT, preferred_element_type=F32)
    if len(o_refs) == 1 and len(o_refs[0].shape) == 2:
        o_refs[0][...] = y
        return
    for s, o_ref in enumerate(o_refs):
        @pl.when(j == s)
        def _():
            for n in range(o_ref.shape[0]):
                o_ref[n] = y[:, n * LANES:(n + 1) * LANES]


def _normed_matmul(x, g, w, *, tn, name):
    t, d = x.shape
    n = w.shape[0]
    assert n % tn == 0 and tn % LANES == 0
    tm = _pick(t, 1024)
    return pl.pallas_call(
        _nmm_kernel,
        grid=(t // tm, n // tn),
        in_specs=[
            pl.BlockSpec((tm, d), lambda i, j: (i, 0)),
            pl.BlockSpec((1, d), lambda i, j: (0, 0)),
            pl.BlockSpec((tn, d), lambda i, j: (j, 0)),
        ],
        out_specs=pl.BlockSpec((tm, tn), lambda i, j: (i, j)),
        out_shape=jax.ShapeDtypeStruct((t, n), F32),
        scratch_shapes=[pltpu.VMEM((tm, d), BF16)],
        compiler_params=_params("parallel", "arbitrary"),
        name=name,
    )(x, g.reshape(1, d), w)


def _normed_matmul_heads(x, g, w, *, sections, batch, seq, name):
    t, d = x.shape
    tn = w.shape[0] // sections
    heads = tn // LANES
    assert t == batch * seq and tn % LANES == 0
    tm = _pick(seq, 1024)
    nt = seq // tm
    out_spec = pl.BlockSpec((None, heads, tm, LANES), lambda i, j: (i // nt, 0, i % nt, 0))
    return pl.pallas_call(
        _nmm_kernel,
        grid=(t // tm, sections),
        in_specs=[
            pl.BlockSpec((tm, d), lambda i, j: (i, 0)),
            pl.BlockSpec((1, d), lambda i, j: (0, 0)),
            pl.BlockSpec((tn, d), lambda i, j: (j, 0)),
        ],
        out_specs=[out_spec] * sections,
        out_shape=[jax.ShapeDtypeStruct((batch, heads, seq, LANES), F32)] * sections,
        scratch_shapes=[pltpu.VMEM((tm, d), BF16)],
        compiler_params=_params("parallel", "arbitrary"),
        name=name,
    )(x, g.reshape(1, d), w)


LOG2_E = math.log2(math.e)


def _sb_suffix(z, valid, u):
    sp = jnp.maximum(z, 0.0) + jnp.log(1.0 + jnp.exp2(jnp.abs(z) * -LOG2_E))
    if valid is not None:
        sp = jnp.where(valid, sp, 0.0)
    return _dot(sp.astype(BF16), u)


def _sb_exp(z, r, valid):
    w = jnp.exp(z - r)
    if valid is not None:
        w = jnp.where(valid, w, 0.0)
    return w.astype(BF16)


def _suffix_matrix(w_keys, totals=False):
    n = w_keys + (LANES if totals else 0)
    j = lax.broadcasted_iota(jnp.int32, (w_keys, n), 0)
    s = lax.broadcasted_iota(jnp.int32, (w_keys, n), 1)
    return ((j >= s) | (s >= w_keys)).astype(BF16)


def _sb_prompt_kernel(bias_ref, q_ref, k_ref, v_ref, u_ref, o_ref, kbf, vbf, acc, car, *, scale, blk, unroll):
    h = pl.program_id(1)
    i = pl.program_id(2)

    @pl.when(i == 0)
    def _():
        kbf[...] = k_ref[...].astype(BF16)
        vbf[...] = v_ref[...].astype(BF16)

    q = (q_ref[...] * scale).astype(BF16)
    bias = bias_ref[h]
    u = u_ref[...]
    rows = lax.broadcasted_iota(jnp.int32, (blk, blk), 0)
    cols = lax.broadcasted_iota(jnp.int32, (blk, blk), 1)

    def blocks(js, diagonal, carry):
        offs = [pl.multiple_of(j * blk, blk) for j in js]
        valids = [cols < rows if diagonal and n == 0 else None for n in range(len(js))]
        zs = [lax.dot_general(q, kbf[pl.ds(off, blk), :], _NT, preferred_element_type=F32) + bias
              for off in offs]
        sufs = [_sb_suffix(z, valid, u) for z, valid in zip(zs, valids)]
        pv = None
        for off, z, suf, valid in zip(offs, zs, sufs, valids):
            r = suf + jnp.concatenate([carry] * (blk // LANES), axis=1)
            carry = jnp.broadcast_to(r[:, 0:1], carry.shape)
            term = _dot(_sb_exp(z, r, valid), vbf[pl.ds(off, blk), :])
            pv = term if pv is None else pv + term
        return pv, carry

    for r in range(2):
        @pl.when(i % 2 == r)
        def _():
            pv, c = blocks([i - m for m in range(r + 1)], True, jnp.zeros((blk, LANES), F32))
            acc[...] = pv
            car[...] = c

    def group(j, size):
        pv, c = blocks([j - m for m in range(size)], False, car[...])
        acc[...] += pv
        car[...] = c

    j = i - 1 - i % 2
    size = 2
    while size < unroll:
        bit = (i // size) % 2

        @pl.when(bit == 1)
        def _():
            group(j, size)

        j = j - size * bit
        size *= 2

    def body(n, _):
        group(j - unroll * n, unroll)
        return 0

    lax.fori_loop(0, i // unroll, body, 0)
    o_ref[...] = acc[...].astype(o_ref.dtype)


def _sb_prompt(q, k, v, bias):
    batch, heads, seq, _ = q.shape
    blk = _pick(seq, 256)
    assert blk % LANES == 0 and seq % blk == 0
    nq = seq // blk
    whole = pl.BlockSpec((None, None, seq, LANES), lambda b, h, i: (b, h, 0, 0))
    return pl.pallas_call(
        functools.partial(_sb_prompt_kernel, scale=1.0 / math.sqrt(LANES), blk=blk, unroll=16),
        grid=(batch, heads, nq),
        in_specs=[
            pl.BlockSpec(memory_space=pltpu.SMEM),
            pl.BlockSpec((None, None, blk, LANES), lambda b, h, i: (b, h, i, 0)),
            whole, whole,
            pl.BlockSpec((blk, blk), lambda b, h, i: (0, 0)),
        ],
        out_specs=pl.BlockSpec((blk, LANES), lambda b, h, i: (b * nq + i, h)),
        out_shape=jax.ShapeDtypeStruct((batch * seq, heads * LANES), BF16),
        scratch_shapes=[
            pltpu.VMEM((seq, LANES), BF16),
            pltpu.VMEM((seq, LANES), BF16),
            pltpu.VMEM((blk, LANES), F32),
            pltpu.VMEM((blk, LANES), F32),
        ],
        compiler_params=_params("parallel", "parallel", "arbitrary"),
        name="sb_prompt",
    )(bias, q, k, v, _suffix_matrix(blk))


def _sb_sample_kernel(pt_ref, q_ref, bias_ref, tpos_ref, u_ref, kn_ref, vn_ref, *rest, pages):
    k_refs = rest[:pages]
    v_refs = rest[pages:2 * pages]
    o_ref, acc, car = rest[2 * pages:]
    s = pl.program_id(1)
    heads, rph, _ = q_ref.shape
    bias = bias_ref[...]
    u = u_ref[...]

    def logits(k_ref):
        return jnp.concatenate(
            [lax.dot_general(q_ref[h], k_ref[h].astype(BF16), _NT, preferred_element_type=F32)
             for h in range(heads)], axis=0) + bias

    def weighted(z, rc, carry, v_ref, valid):
        w = _sb_exp(z, rc[:, :LANES] + carry, valid)
        return jnp.concatenate(
            [_dot(w[h * rph:(h + 1) * rph], v_ref[h].astype(BF16)) for h in range(heads)], axis=0)

    @pl.when(s == 0)
    def _():
        valid = lax.broadcasted_iota(jnp.int32, tpos_ref.shape, 1) < tpos_ref[...]
        z = logits(kn_ref)
        rc = _sb_suffix(z, valid, u)
        acc[...] = weighted(z, rc, jnp.zeros_like(z), vn_ref, valid)
        car[...] = rc[:, LANES:]

    zs = [logits(k_refs[p]) for p in range(pages)]
    rows = zs[0].shape[0]
    rc_all = _sb_suffix(jnp.concatenate(zs, axis=0), None, u)
    rcs = [rc_all[p * rows:(p + 1) * rows] for p in range(pages)]
    carry = car[...]
    carries = []
    for rc in rcs:
        carries.append(carry)
        carry = carry + rc[:, LANES:]
    car[...] = carry
    pvs = [weighted(z, rc, c, v_refs[p], None) for p, (z, rc, c) in enumerate(zip(zs, rcs, carries))]
    acc[...] += functools.reduce(lambda a, b: a + b, pvs)

    @pl.when(s == pl.num_programs(1) - 1)
    def _():
        o_ref[...] = acc[...]


def _sb_sample(q, k_new, v_new, cache_k, cache_v, page_table, bias):
    h, b, t, _ = q.shape
    n_phys, page = cache_k.shape[:2]
    n_pages = page_table.shape[1]
    assert page == LANES and t <= page
    rph = _round_up(t, SUBLANES)
    rows = h * rph
    pages = _divisor(n_pages, 16)
    scale = 1.0 / math.sqrt(LANES)
    per_head = lambda a, n: jnp.pad(a.transpose(1, 0, 2, 3), ((0, 0), (0, 0), (0, n - t), (0, 0)))
    q_rows = per_head(q * scale, rph).astype(BF16)
    bias_rows = jnp.broadcast_to(jnp.repeat(bias.astype(F32), rph)[:, None], (rows, LANES))
    tpos = jnp.broadcast_to(jnp.tile(jnp.arange(rph, dtype=jnp.int32), h)[:, None], (rows, LANES))
    tpos = jnp.minimum(tpos, t)
    ck = cache_k.transpose(0, 2, 1, 3)
    cv = cache_v.transpose(0, 2, 1, 3)

    def page_spec(p):
        return pl.BlockSpec((None, h, page, LANES),
                            lambda i, s, pt: (pt[i, n_pages - 1 - (s * pages + p)], 0, 0, 0))

    const = lambda shape: pl.BlockSpec(shape, lambda i, s, pt: (0,) * len(shape))
    per_b = lambda *shape: pl.BlockSpec((None,) + shape, lambda i, s, pt: (i,) + (0,) * len(shape))
    o_rows = pl.pallas_call(
        functools.partial(_sb_sample_kernel, pages=pages),
        grid_spec=pltpu.PrefetchScalarGridSpec(
            num_scalar_prefetch=1,
            grid=(b, n_pages // pages),
            in_specs=[per_b(h, rph, LANES), const((rows, LANES)), const((rows, LANES)),
                      const((page, page + LANES)), per_b(h, page, LANES), per_b(h, page, LANES)]
                     + [page_spec(p) for p in range(pages)] * 2,
            out_specs=per_b(rows, LANES),
            scratch_shapes=[pltpu.VMEM((rows, LANES), F32), pltpu.VMEM((rows, LANES), F32)],
        ),
        out_shape=jax.ShapeDtypeStruct((b, rows, LANES), F32),
        compiler_params=_params("parallel", "arbitrary"),
        name="sb_sample",
    )(page_table, q_rows, bias_rows, tpos, _suffix_matrix(page, totals=True),
      per_head(k_new, page), per_head(v_new, page),
      *([ck] * pages), *([cv] * pages))
    return o_rows.reshape(b, h, rph, LANES)[:, :, :t].transpose(0, 2, 1, 3)


def _dot_hi(a, b):
    return jnp.dot(a, b, precision=HIGHEST, preferred_element_type=F32)


def _split(a):
    hi = a.astype(BF16)
    return hi, (a - hi.astype(F32)).astype(BF16)


def _dot3(a, b):
    return _dot(a[0], b[0]) + _dot(a[0], b[1]) + _dot(a[1], b[0])


def _gdn_prep_kernel(raw_ref, prev_ref, buf_ref, a_ref, b_ref, cw_ref, alog_ref, dtb_ref,
                     u_ref, w_ref, qd_ref, kd_ref, qk_ref, egl_ref, xfull,
                     *, heads, conv_k, chunk, valid_last):
    c = pl.program_id(1)
    rows = raw_ref.shape[0]
    n_chunks = rows // chunk
    gw = heads * LANES

    @pl.when(c == 0)
    def _():
        xfull[0:SUBLANES, :] = buf_ref[...]

    @pl.when(c > 0)
    def _():
        xfull[0:SUBLANES, :] = prev_ref[...]

    xfull[SUBLANES:SUBLANES + rows, :] = raw_ref[...]
    x = xfull[...]
    y = x[SUBLANES:] * cw_ref[conv_k - 1:conv_k, :]
    for j in range(conv_k - 1):
        back = pltpu.roll(x, conv_k - 1 - j, axis=0)
        y = y + back[SUBLANES:] * cw_ref[j:j + 1, :]
    y = _silu(y)

    ones = jnp.ones((LANES, LANES), BF16)

    def l2norm(a):
        ss = _dot((a * a).astype(BF16), ones)
        return a * lax.rsqrt(ss + NORM_EPS)

    qn = [l2norm(y[:, h * LANES:(h + 1) * LANES]) * (LANES ** -0.5) for h in range(heads)]
    kn = [l2norm(y[:, gw + h * LANES:gw + (h + 1) * LANES]) for h in range(heads)]

    g = -jnp.exp(alog_ref[...]) * _softplus(a_ref[...] + dtb_ref[...])
    beta = jax.nn.sigmoid(b_ref[...])
    live = None
    if valid_last < chunk:
        last = c == pl.num_programs(1) - 1
        row = lax.broadcasted_iota(jnp.int32, (rows, 1), 0)
        live_rows = row < jnp.where(last, rows - chunk + valid_last, rows)
        g = jnp.where(live_rows, g, 0.0)
        beta = jnp.where(live_rows, beta, 0.0)
        live = lax.broadcasted_iota(jnp.int32, (chunk, 1), 0) < jnp.where(last, valid_last, chunk)

    shift = int(math.log2(chunk))
    ri = lax.broadcasted_iota(jnp.int32, (rows, rows), 0)
    ci = lax.broadcasted_iota(jnp.int32, (rows, rows), 1)
    same = (ri >> shift) == (ci >> shift)
    gc = _dot_hi(jnp.where(same & (ri >= ci), 1.0, 0.0), g)
    sel = (lax.broadcasted_iota(jnp.int32, (SUBLANES, LANES), 0)
           == lax.broadcasted_iota(jnp.int32, (SUBLANES, LANES), 1)).astype(F32)
    gc_rows = lax.dot_general(sel, gc, _NT, precision=HIGHEST, preferred_element_type=F32)

    ri = lax.broadcasted_iota(jnp.int32, (chunk, chunk), 0)
    ci = lax.broadcasted_iota(jnp.int32, (chunk, chunk), 1)
    incl = ri >= ci
    strict = ri > ci

    for n in range(n_chunks):
        egl_ref[n] = jnp.broadcast_to(jnp.exp(gc[(n + 1) * chunk - 1:(n + 1) * chunk, :]), (SUBLANES, LANES))

    chains = [(n, h) for n in range(n_chunks) for h in range(heads)]
    lows, rhss = [], []
    for n, h in chains:
        rs = slice(n * chunk, (n + 1) * chunk)
        sl = slice(h * LANES, (h + 1) * LANES)
        qh = qn[h][rs]
        kh = kn[h][rs]
        vh = y[rs, 2 * gw + h * LANES:2 * gw + (h + 1) * LANES]
        if live is not None and n == n_chunks - 1:
            qh = jnp.where(live, qh, 0.0)
            kh = jnp.where(live, kh, 0.0)
            vh = jnp.where(live, vh, 0.0)
        g_col = gc[rs, h:h + 1]
        g_row = gc_rows[h:h + 1, rs]
        g_last = gc[(n + 1) * chunk - 1:(n + 1) * chunk, h:h + 1]
        decay = jnp.exp(jnp.where(incl, g_col - g_row, 0.0))
        b_col = beta[rs, h:h + 1]
        eg = jnp.exp(g_col)
        kb = kh * b_col
        k_bf = kh.astype(BF16)
        low = lax.dot_general(kb.astype(BF16), k_bf, _NT, preferred_element_type=F32)
        lows.append(low * jnp.where(strict, decay, 0.0))
        rhss.append(_split(jnp.concatenate([vh * b_col, kb * eg], axis=1)))
        qk = lax.dot_general(qh.astype(BF16), k_bf, _NT, preferred_element_type=F32)
        qk_ref[rs, h * chunk:(h + 1) * chunk] = qk * jnp.where(incl, decay, 0.0)
        qd_ref[rs, sl] = (qh * eg).astype(BF16)
        kd_ref[rs, sl] = (kh * jnp.exp(g_last - g_col)).astype(BF16)

    eye_c = jnp.where(ri == ci, 1.0, 0.0)
    pws = [_split(low) for low in lows]
    invs = [eye_c - low for low in lows]
    for _ in range(shift - 1):
        pws = [_split(_dot3(pw, pw)) for pw in pws]
        invs = [inv + _dot3(_split(inv), pw) for inv, pw in zip(invs, pws)]
    for (n, h), inv, rhs in zip(chains, invs, rhss):
        rs = slice(n * chunk, (n + 1) * chunk)
        sl = slice(h * LANES, (h + 1) * LANES)
        x = _dot3(_split(inv), rhs)
        u_ref[rs, sl] = x[:, :LANES]
        w_ref[rs, sl] = x[:, LANES:].astype(BF16)


def _gdn_scan_kernel(u_ref, w_ref, qd_ref, kd_ref, qk_ref, egl_ref, z_ref, s0_ref, on_ref,
                     o_ref, s_ref, *, heads, chunk):
    bt, rows, _ = u_ref.shape

    @pl.when(pl.program_id(1) == 0)
    def _():
        s_ref[...] = s0_ref[...]

    chains = [(bi, h, slice(h * LANES, (h + 1) * LANES)) for bi in range(bt) for h in range(heads)]
    for n in range(rows // chunk):
        rs = slice(n * chunk, (n + 1) * chunk)
        s_old = [s_ref[bi, h] for bi, h, _ in chains]
        s_bf = [s.astype(BF16) for s in s_old]
        ws = [_dot(w_ref[bi, rs, sl], s) for (bi, _, sl), s in zip(chains, s_bf)]
        qs = [_dot(qd_ref[bi, rs, sl], s) for (bi, _, sl), s in zip(chains, s_bf)]
        v_new = [(u_ref[bi, rs, sl] - x).astype(BF16) for (bi, _, sl), x in zip(chains, ws)]
        os = [x + _dot(qk_ref[bi, rs, h * chunk:(h + 1) * chunk].astype(BF16), v)
              for (bi, h, _), x, v in zip(chains, qs, v_new)]
        upd = [lax.dot_general(kd_ref[bi, rs, sl], v, _TN, preferred_element_type=F32)
               for (bi, _, sl), v in zip(chains, v_new)]
        for (bi, h, sl), s, d, o_h in zip(chains, s_old, upd, os):
            s_ref[bi, h] = s * egl_ref[bi, n, 0:1, h:h + 1] + d
            o_h = _rmsnorm(o_h, on_ref[...]) * _silu(z_ref[bi, rs, sl])
            o_ref[bi, rs, sl] = o_h.astype(o_ref.dtype)


def _gdn(raw, a, b, z, conv_buf, s0, conv_w, a_log, dt_bias, out_norm, *, batch, seq, valid_last):
    heads = s0.shape[1]
    gw = heads * LANES
    conv_k = conv_w.shape[0]
    chunk = GDN_CHUNK
    assert seq % chunk == 0 and heads <= SUBLANES and conv_k - 1 <= SUBLANES
    nc = seq // chunk
    (raw_a, raw_c), (a_a, a_c), (b_a, b_c), (z_a, z_c) = raw, a, b, z
    lane_vec = lambda v: jnp.pad(v.astype(F32), (0, LANES - heads)).reshape(1, LANES)
    buf = jnp.pad(conv_buf, ((0, 0), (SUBLANES - (conv_k - 1), 0), (0, 0)))

    cps = _divisor(nc, 4)
    rows = cps * chunk
    ns = nc // cps
    tok = lambda width, col: pl.BlockSpec((rows, width), lambda i, c: (i * ns + c, col))
    const = lambda shape: pl.BlockSpec(shape, lambda i, c: (0,) * len(shape))
    tokens = lambda width, dtype: jax.ShapeDtypeStruct((batch * seq, width), dtype)
    u, w, qd, kd, qk, egl = pl.pallas_call(
        functools.partial(_gdn_prep_kernel, heads=heads, conv_k=conv_k, chunk=chunk, valid_last=valid_last),
        grid=(batch, ns),
        in_specs=[
            tok(3 * gw, raw_c),
            pl.BlockSpec((SUBLANES, 3 * gw),
                         lambda i, c: (jnp.maximum((i * ns + c) * (rows // SUBLANES) - 1, 0), raw_c)),
            pl.BlockSpec((None, SUBLANES, 3 * gw), lambda i, c: (i, 0, 0)),
            tok(LANES, a_c), tok(LANES, b_c),
            const((conv_k, 3 * gw)), const((1, LANES)), const((1, LANES)),
        ],
        out_specs=[tok(gw, 0), tok(gw, 0), tok(gw, 0), tok(gw, 0), tok(heads * chunk, 0),
                   pl.BlockSpec((cps, SUBLANES, LANES), lambda i, c: (i * ns + c, 0, 0))],
        out_shape=[tokens(gw, F32), tokens(gw, BF16), tokens(gw, BF16), tokens(gw, BF16),
                   tokens(heads * chunk, F32),
                   jax.ShapeDtypeStruct((batch * nc, SUBLANES, LANES), F32)],
        scratch_shapes=[pltpu.VMEM((SUBLANES + rows, 3 * gw), F32)],
        compiler_params=_params("parallel", "arbitrary"),
        name="gdn_prep",
    )(raw_a, raw_a, buf, a_a, b_a, conv_w, lane_vec(a_log), lane_vec(dt_bias))

    bt = 2 if batch % 2 == 0 else 1
    nz = z_a.shape[1]
    seq3 = lambda arr: arr.reshape(batch, seq, arr.shape[-1])
    tok3 = lambda width, col=0: pl.BlockSpec((bt, rows, width), lambda i, c: (i, c, col))
    state = pl.BlockSpec((bt, heads, LANES, LANES), lambda i, c: (i, 0, 0, 0))
    o, s_new = pl.pallas_call(
        functools.partial(_gdn_scan_kernel, heads=heads, chunk=chunk),
        grid=(batch // bt, ns),
        in_specs=[tok3(gw), tok3(gw), tok3(gw), tok3(gw), tok3(heads * chunk),
                  pl.BlockSpec((bt, cps, SUBLANES, LANES), lambda i, c: (i, c, 0, 0)),
                  tok3(gw, z_c), state, const((1, LANES))],
        out_specs=[tok3(gw), state],
        out_shape=[jax.ShapeDtypeStruct((batch, seq, gw), BF16), jax.ShapeDtypeStruct(s0.shape, F32)],
        compiler_params=_params("parallel", "arbitrary"),
        name="gdn_scan",
    )(seq3(u), seq3(w), seq3(qd), seq3(kd), seq3(qk), egl.reshape(batch, nc, SUBLANES, LANES),
      z_a.reshape(batch, seq, nz), s0.astype(F32), out_norm.astype(F32).reshape(1, LANES))
    return o.reshape(batch * seq, gw), s_new


def _mem_attn_kernel(q_ref, k_ref, v_ref, o_ref, *, scale):
    for h in range(k_ref.shape[0]):
        sl = slice(h * LANES, (h + 1) * LANES)
        q = (q_ref[:, sl] * scale).astype(BF16)
        z = lax.dot_general(q, k_ref[h].astype(BF16), _NT, preferred_element_type=F32)
        e = jnp.exp(z - jnp.max(z, axis=-1, keepdims=True))
        p = e / jnp.sum(e, axis=-1, keepdims=True)
        o_ref[:, sl] = _dot(p.astype(BF16), v_ref[h].astype(BF16)).astype(o_ref.dtype)


def _mem_attn(q, mem_k, mem_v, *, batch, seq):
    q_a, q_c = q
    _, heads, n_mem, _ = mem_k.shape
    xw = heads * LANES
    tq = _pick(seq, 512)
    nq = seq // tq
    mem_spec = pl.BlockSpec((None, heads, n_mem, LANES), lambda i, j: (i, 0, 0, 0))
    return pl.pallas_call(
        functools.partial(_mem_attn_kernel, scale=1.0 / math.sqrt(LANES)),
        grid=(batch, nq),
        in_specs=[pl.BlockSpec((tq, xw), lambda i, j: (i * nq + j, q_c)), mem_spec, mem_spec],
        out_specs=pl.BlockSpec((tq, xw), lambda i, j: (i * nq + j, 0)),
        out_shape=jax.ShapeDtypeStruct((batch * seq, xw), BF16),
        compiler_params=_params("parallel", "parallel"),
        name="mem_attn",
    )(q_a, mem_k, mem_v)


def _merge_kernel(x_ref, osb_ref, ogdn_ref, ox_ref, gsb_ref, ggdn_ref, gx_ref,
                  wsb_ref, wgdn_ref, wx_ref, wout_ref, o_ref):
    merged = (jax.nn.sigmoid(gsb_ref[...]) * _dot(osb_ref[...], wsb_ref[...])
              + jax.nn.sigmoid(ggdn_ref[...]) * _dot(ogdn_ref[...], wgdn_ref[...])
              + jax.nn.sigmoid(gx_ref[...]) * _dot(ox_ref[...], wx_ref[...]))
    o_ref[...] = x_ref[...] + _dot(merged.astype(BF16), wout_ref[...])


def _merge(x, o_sb, o_gdn, o_x, proj, gate_col, w_up_sb, w_up_gdn, w_up_x, w_out):
    t, d = x.shape
    tm = _pick(t, 256)
    tok = lambda a: pl.BlockSpec((tm, a.shape[1]), lambda i: (i, 0))
    gate = lambda n: pl.BlockSpec((tm, d), lambda i: (i, gate_col + n))
    weight = lambda w: pl.BlockSpec(w.shape, lambda i: (0, 0), pipeline_mode=pl.Buffered(1))
    return pl.pallas_call(
        _merge_kernel,
        grid=(t // tm,),
        in_specs=[tok(x), tok(o_sb), tok(o_gdn), tok(o_x), gate(0), gate(1), gate(2),
                  weight(w_up_sb), weight(w_up_gdn), weight(w_up_x), weight(w_out)],
        out_specs=tok(x),
        out_shape=jax.ShapeDtypeStruct((t, d), F32),
        compiler_params=_params("parallel"),
        name="merge",
    )(x, o_sb, o_gdn, o_x, proj, proj, proj, w_up_sb, w_up_gdn, w_up_x, w_out)


def _split_w_in(w_in, sbw, gw, xw, d, g_heads):
    wt = w_in.T
    o = 3 * sbw
    w_gqkv = wt[o:o + 3 * gw]
    o += 3 * gw
    w_a = wt[o:o + g_heads]
    w_b = wt[o + g_heads:o + 2 * g_heads]
    o += 2 * g_heads
    w_z = wt[o:o + gw]
    w_xq = wt[o + gw:o + gw + xw]
    gates = wt[o + gw + xw:]
    assert gates.shape[0] == 3 * d
    pad = lambda w: jnp.pad(w, ((0, LANES - g_heads), (0, 0)))
    rest = jnp.concatenate([gates, w_z, w_gqkv, w_xq, pad(w_a), pad(w_b)], axis=0)
    return wt[:3 * sbw].astype(BF16), rest.astype(BF16)


def kernel(x_prompt, x_sample, cache_sb_k, cache_sb_v, cache_mem_k, cache_mem_v, state_gdn_S, state_gdn_conv, page_table, mem_prompt, ffn1_norm, ffn1_w_in, ffn1_w_out, mix_norm, w_in, sb_logit_bias, gdn_conv_w, gdn_a_log, gdn_dt_bias, gdn_out_norm, mem_norm, w_mem_kv, w_up_sb, w_up_gdn, w_up_x, w_out, ffn2_norm, ffn2_w_in, ffn2_w_out, final_norm):
    batch, seq, d = x_prompt.shape
    dec_batch, dec_seq, _ = x_sample.shape
    depth = w_in.shape[0]
    sb_heads = cache_sb_k.shape[3]
    x_heads = cache_mem_k.shape[3]
    g_heads = state_gdn_S.shape[2]
    n_mem = mem_prompt.shape[1]
    conv_k = gdn_conv_w.shape[1]
    sbw, gw, xw = sb_heads * LANES, g_heads * LANES, x_heads * LANES

    off_gate, off_z = 0, 3 * d
    off_gdn = off_z + gw
    off_xq = off_gdn + 3 * gw
    off_a, off_b = off_xq + xw, off_xq + xw + LANES
    n_proj = off_b + LANES
    assert off_z % gw == 0 and off_gdn % (3 * gw) == 0 and off_xq % xw == 0
    tn_proj = 3 * 256
    assert n_proj % tn_proj == 0

    xp = x_prompt.reshape(batch * seq, d)
    xs = x_sample.reshape(dec_batch * dec_seq, d)
    outs = [[] for _ in range(10)]
    for l in range(depth):
        bf = lambda w: w[l].astype(BF16)
        w1_in, w1_out, w2_in, w2_out = bf(ffn1_w_in), bf(ffn1_w_out), bf(ffn2_w_in), bf(ffn2_w_out)
        w_sb, w_rest = _split_w_in(w_in[l], sbw, gw, xw, d, g_heads)
        wup_sb, wup_gdn, wup_x, wo = bf(w_up_sb), bf(w_up_gdn), bf(w_up_x), bf(w_out)
        last = l == depth - 1
        mixer_w = (gdn_conv_w[l], gdn_a_log[l], gdn_dt_bias[l], gdn_out_norm[l])

        def mix_tail(x, proj, o_sb, o_gdn, o_x):
            x = _merge(x, o_sb, o_gdn, o_x, proj, off_gate // d, wup_sb, wup_gdn, wup_x, wo)
            return _ffn(x, ffn2_norm[l], w2_in, w2_out, final_norm, final=last)

        mk_p, mv_p = _normed_matmul_heads(mem_prompt.reshape(batch * n_mem, d), mem_norm[l], bf(w_mem_kv).T,
                                          sections=2, batch=batch, seq=n_mem, name="mem_kv")
        xp = _ffn(xp, ffn1_norm[l], w1_in, w1_out, final_norm, final=False)
        q_p, k_p, v_p = _normed_matmul_heads(xp, mix_norm[l], w_sb, sections=3, batch=batch, seq=seq,
                                             name="sb_proj")
        proj = _normed_matmul(xp, mix_norm[l], w_rest, tn=tn_proj, name="mixer_proj")
        o_sb = _sb_prompt(q_p, k_p, v_p, sb_logit_bias[l].astype(F32))
        o_gdn, s_p = _gdn((proj, off_gdn // (3 * gw)), (proj, off_a // LANES), (proj, off_b // LANES),
                          (proj, off_z // gw), jnp.zeros((batch, conv_k - 1, 3 * gw), F32),
                          jnp.zeros((batch, g_heads, LANES, LANES), F32), *mixer_w,
                          batch=batch, seq=seq, valid_last=GDN_CHUNK)
        o_x = _mem_attn((proj, off_xq // xw), mk_p, mv_p, batch=batch, seq=seq)
        xp = mix_tail(xp, proj, o_sb, o_gdn, o_x)
        seq_major = lambda a: a.transpose(0, 2, 1, 3)
        outs[0].append(seq_major(k_p))
        outs[1].append(seq_major(v_p))
        outs[2].append(seq_major(mk_p))
        outs[3].append(seq_major(mv_p))
        outs[4].append(s_p)
        assert seq >= conv_k - 1
        outs[5].append(proj.reshape(batch, seq, n_proj)[:, seq - (conv_k - 1):, off_gdn:off_gdn + 3 * gw])

        xs = _ffn(xs, ffn1_norm[l], w1_in, w1_out, final_norm, final=False)
        qkv = _normed_matmul_heads(xs, mix_norm[l], w_sb, sections=3, batch=1, seq=dec_batch * dec_seq,
                                   name="sb_proj_sample")
        proj = _normed_matmul(xs, mix_norm[l], w_rest, tn=tn_proj, name="mixer_proj_sample")
        q_s, k_s, v_s = (a.reshape(sb_heads, dec_batch, dec_seq, LANES) for a in qkv)
        o_sb = _sb_sample(q_s, k_s, v_s, cache_sb_k[l], cache_sb_v[l], page_table, sb_logit_bias[l])
        o_sb = o_sb.reshape(dec_batch * dec_seq, sbw).astype(BF16)
        proj3 = proj.reshape(dec_batch, dec_seq, n_proj)
        seq_g = _round_up(dec_seq, GDN_CHUNK)
        seq_x = _round_up(dec_seq, SUBLANES)
        padded = lambda lo, hi, n: jnp.pad(proj3[:, :, lo:hi], ((0, 0), (0, n - dec_seq), (0, 0))
                                           ).reshape(dec_batch * n, hi - lo)
        raw_s = proj3[:, :, off_gdn:off_gdn + 3 * gw]
        o_gdn, s_s = _gdn((padded(off_gdn, off_gdn + 3 * gw, seq_g), 0), (padded(off_a, off_a + LANES, seq_g), 0),
                          (padded(off_b, off_b + LANES, seq_g), 0), (padded(off_z, off_z + gw, seq_g), 0),
                          state_gdn_conv[l], state_gdn_S[l], *mixer_w,
                          batch=dec_batch, seq=seq_g, valid_last=dec_seq - (seq_g - GDN_CHUNK))
        o_gdn = o_gdn.reshape(dec_batch, seq_g, gw)[:, :dec_seq].reshape(dec_batch * dec_seq, gw)
        o_x = _mem_attn((padded(off_xq, off_xq + xw, seq_x), 0), cache_mem_k[l].transpose(0, 2, 1, 3),
                        cache_mem_v[l].transpose(0, 2, 1, 3), batch=dec_batch, seq=seq_x)
        o_x = o_x.reshape(dec_batch, seq_x, xw)[:, :dec_seq].reshape(dec_batch * dec_seq, xw)
        xs = mix_tail(xs, proj, o_sb, o_gdn, o_x)
        outs[6].append(k_s.transpose(1, 2, 0, 3))
        outs[7].append(v_s.transpose(1, 2, 0, 3))
        outs[8].append(s_s)
        conv_all = jnp.concatenate([state_gdn_conv[l].astype(F32), raw_s], axis=1)
        outs[9].append(conv_all[:, dec_seq:])

    stack = [jnp.stack(o) for o in outs]
    return (xp.reshape(batch, seq, d), xs.reshape(dec_batch, dec_seq, d), *stack)
```
